```python
import numpy as np
import jax
import jax.numpy as jnp
from jax import lax

D_MODEL = 2048
BATCH = 4
SEQ = 4096
DEPTH = 4

GRID_W = 64
CTX_LEN = 256
HEAD_DIM = 128
N_HEADS = D_MODEL // HEAD_DIM
NA_HEADS = N_HEADS // 2
GLA_HEADS = N_HEADS - NA_HEADS
NA_ROWS_MAX = 8
NA_COLS = 16
GLA_DK = HEAD_DIM // 2
GLA_DV = HEAD_DIM
GLA_GATE_RANK = 16
GLA_TAU = 16.0
GLA_CHUNK = 64
POOL_WINDOWS = (2, 4, 8, 16)
POOL_GROUP = D_MODEL // 4
D_FF = 5632
ROPE_BASE = 10000.0
EPS = 1e-6
N_EVEN = (DEPTH + 1) // 2
N_ODD = DEPTH // 2
A_W = NA_HEADS * HEAD_DIM
BQK_W = GLA_HEADS * GLA_DK
BV_W = GLA_HEADS * GLA_DV
IN_SPLITS = (A_W, A_W, A_W, BQK_W, BQK_W, BV_W, BV_W, 2 * GLA_GATE_RANK)
D_IN = 3 * A_W + 2 * BQK_W + 2 * BV_W + 2 * GLA_GATE_RANK

kernel_name = 'hybrid_natten_gla_pool_dit'


def _rmsnorm(x, g):
    xf = x.astype(jnp.float32)
    y = xf * lax.rsqrt(jnp.mean(xf * xf, axis=-1, keepdims=True) + EPS)
    return (y * g.astype(jnp.float32)).astype(x.dtype)


def _modulate(h, shift, scale):
    return h * (1 + scale) + shift


def _split_heads(a, n):
    b, l, _ = a.shape
    return a.reshape(b, l, n, -1).transpose(0, 2, 1, 3)


def _merge_heads(a):
    b, h, l, d = a.shape
    return a.transpose(0, 2, 1, 3).reshape(b, l, h * d)


def _flip(a):
    return a[:, :, ::-1]


def _axial_rope(x, seq_len):
    t = jnp.arange(seq_len)
    row = (t // GRID_W).astype(jnp.float32)
    col = (t % GRID_W).astype(jnp.float32)
    half = x.shape[-1] // 2
    nf = half // 2
    inv = ROPE_BASE ** (-jnp.arange(nf, dtype=jnp.float32) / nf)

    def rot(xh, pos):
        ang = pos[:, None] * inv[None, :]
        cos = jnp.cos(ang).astype(xh.dtype)
        sin = jnp.sin(ang).astype(xh.dtype)
        x1, x2 = xh[..., :nf], xh[..., nf:]
        return jnp.concatenate([x1 * cos - x2 * sin, x2 * cos + x1 * sin], axis=-1)

    return jnp.concatenate([rot(x[..., :half], row), rot(x[..., half:], col)], axis=-1)


def _dense_attention(q, k, v):
    s = jnp.einsum('bhqd,bhkd->bhqk', q, k).astype(jnp.float32) * (q.shape[-1] ** -0.5)
    p = jax.nn.softmax(s, axis=-1).astype(v.dtype)
    return jnp.einsum('bhqk,bhkd->bhqd', p, v)


def _neighbourhood_attention(q, k, v, k_ctx, v_ctx, rpb):
    b, h, seq_len, dh = q.shape
    rows = seq_len // GRID_W
    wr = min(NA_ROWS_MAX, rows)
    band = wr * GRID_W
    scale = dh ** -0.5
    qcol = jnp.arange(GRID_W)
    kcol = jnp.arange(GRID_W)
    cstart = jnp.clip(qcol - NA_COLS // 2, 0, GRID_W - NA_COLS)
    col_in = (kcol[None, :] >= cstart[:, None]) & (kcol[None, :] < cstart[:, None] + NA_COLS)
    mask = jnp.broadcast_to(col_in[:, None, :], (GRID_W, wr, GRID_W)).reshape(GRID_W, band)
    dc_idx = jnp.clip(kcol[None, :] - qcol[:, None] + NA_COLS - 1, 0, 2 * NA_COLS - 2)
    col_bias = rpb[:, :, dc_idx]

    def row_block(r):
        rstart = jnp.clip(r - wr // 2, 0, rows - wr)
        qr = lax.dynamic_slice_in_dim(q, r * GRID_W, GRID_W, axis=2)
        kb = lax.dynamic_slice_in_dim(k, rstart * GRID_W, band, axis=2)
        vb = lax.dynamic_slice_in_dim(v, rstart * GRID_W, band, axis=2)
        dr_idx = rstart + jnp.arange(wr) - r + NA_ROWS_MAX - 1
        bias = col_bias[:, dr_idx].transpose(0, 2, 1, 3).reshape(h, GRID_W, band)
        s_loc = jnp.einsum('bhqd,bhkd->bhqk', qr, kb).astype(jnp.float32) * scale + bias[None].astype(jnp.float32)
        s_loc = jnp.where(mask, s_loc, -jnp.inf)
        s_ctx = jnp.einsum('bhqd,bhkd->bhqk', qr, k_ctx).astype(jnp.float32) * scale
        p = jax.nn.softmax(jnp.concatenate([s_loc, s_ctx], axis=-1), axis=-1).astype(v.dtype)
        return (jnp.einsum('bhqk,bhkd->bhqd', p[..., :band], vb)
                + jnp.einsum('bhqk,bhkd->bhqd', p[..., band:], v_ctx))

    out = lax.map(row_block, jnp.arange(rows))
    return out.transpose(1, 2, 0, 3, 4).reshape(b, h, seq_len, dh)


def _gla_chunked(q, k, v, g, s0):
    b, h, seq_len, dk = q.shape
    dv = v.shape[-1]
    n = seq_len // GLA_CHUNK
    out_dtype = v.dtype

    def to_chunks(a):
        return a.astype(jnp.float32).reshape(b, h, n, GLA_CHUNK, a.shape[-1]).transpose(2, 0, 1, 3, 4)

    lower = jnp.tril(jnp.ones((GLA_CHUNK, GLA_CHUNK), dtype=bool))

    def step(s, inp):
        qc, kc, vc, gc = inp
        cum = jnp.cumsum(gc, axis=2)
        o_inter = jnp.einsum('bhcd,bhde->bhce', qc * jnp.exp(cum), s)
        diff = cum[:, :, :, None, :] - cum[:, :, None, :, :]
        decay = jnp.where(lower[:, :, None], jnp.exp(jnp.minimum(diff, 0.0)), 0.0)
        attn = jnp.einsum('bhid,bhjd,bhijd->bhij', qc, kc, decay)
        o_intra = jnp.einsum('bhij,bhje->bhie', attn, vc)
        last = cum[:, :, -1:, :]
        s_new = jnp.exp(last[:, :, 0, :])[..., None] * s + jnp.einsum('bhcd,bhce->bhde', kc * jnp.exp(last - cum), vc)
        return s_new, o_inter + o_intra

    s_fin, o = lax.scan(step, s0, (to_chunks(q), to_chunks(k), to_chunks(v), to_chunks(g)))
    o = o.transpose(1, 2, 0, 3, 4).reshape(b, h, seq_len, dv)
    return o.astype(out_dtype), s_fin


def _gla_final_state(k, v, g):
    cum = jnp.cumsum(g, axis=2)
    return jnp.einsum('bhld,bhle->bhde', k.astype(jnp.float32) * jnp.exp(cum[:, :, -1:] - cum), v.astype(jnp.float32))


def _project_even(h, w_in, w_gate2, b_gate):
    p = h @ w_in
    offs = np.cumsum(IN_SPLITS)[:-1].tolist()
    qa, ka, va, qb, kb, vb, gb, ab = jnp.split(p, offs, axis=-1)
    log_gates = []
    for d in range(2):
        z = ab[..., d * GLA_GATE_RANK:(d + 1) * GLA_GATE_RANK] @ w_gate2[d] + b_gate[d]
        log_gates.append(_split_heads(jax.nn.log_sigmoid(z.astype(jnp.float32)) / GLA_TAU, GLA_HEADS))
    return (_split_heads(qa, NA_HEADS), _split_heads(ka, NA_HEADS), _split_heads(va, NA_HEADS),
            _split_heads(qb, GLA_HEADS), _split_heads(kb, GLA_HEADS), _split_heads(vb, GLA_HEADS),
            gb, log_gates[0], log_gates[1])


def _combine(oa, ob, gb, gla_g, w_out):
    ob = _rmsnorm(ob, gla_g[:, None, :])
    ob = _merge_heads(ob) * jax.nn.silu(gb)
    return jnp.concatenate([_merge_heads(oa), ob], axis=-1) @ w_out


def _even_mixer(h_lat, h_ctx, w_in, w_gate2, b_gate, rpb, gla_g, w_out, need_ctx):
    qa, ka, va, qb, kb, vb, gb, lgf, lgb = _project_even(h_lat, w_in, w_gate2, b_gate)
    qa_c, ka_c, va_c, qb_c, kb_c, vb_c, gb_c, lgf_c, lgb_c = _project_even(h_ctx, w_in, w_gate2, b_gate)
    seq_len = h_lat.shape[1]
    bsz = h_lat.shape[0]
    oa = _neighbourhood_attention(qa, ka, va, ka_c, va_c, rpb)
    qs = GLA_DK ** -0.5
    qb_r = _axial_rope(qb, seq_len) * qs
    kb_r = _axial_rope(kb, seq_len)
    s0 = jnp.zeros((bsz, GLA_HEADS, GLA_DK, GLA_DV), jnp.float32)
    if need_ctx:
        ob_cf, s_f = _gla_chunked(qb_c * qs, kb_c, vb_c, lgf_c, s0)
        ob_cb, s_b = _gla_chunked(_flip(qb_c * qs), _flip(kb_c), _flip(vb_c), _flip(lgb_c), s0)
    else:
        s_f = _gla_final_state(kb_c, vb_c, lgf_c)
        s_b = _gla_final_state(_flip(kb_c), _flip(vb_c), _flip(lgb_c))
    ob_f, _ = _gla_chunked(qb_r, kb_r, vb, lgf, s_f)
    ob_b, _ = _gla_chunked(_flip(qb_r), _flip(kb_r), _flip(vb), _flip(lgb), s_b)
    y_lat = _combine(oa, ob_f + _flip(ob_b), gb, gla_g, w_out)
    y_ctx = None
    if need_ctx:
        oa_c = _dense_attention(qa_c, ka_c, va_c)
        y_ctx = _combine(oa_c, ob_cf + _flip(ob_cb), gb_c, gla_g, w_out)
    return y_lat, y_ctx


def _pool_mix(h, w_pool, pool_scale):
    b, seq_len, d = h.shape
    hf = h.astype(jnp.float32)
    cs = jnp.concatenate([jnp.zeros((b, 1, d), jnp.float32), jnp.cumsum(hf, axis=1)], axis=1)
    t = jnp.arange(seq_len)
    groups = []
    for gi, w in enumerate(POOL_WINDOWS):
        lo = jnp.clip(t - w // 2, 0, seq_len)
        hi = jnp.clip(t + w // 2, 0, seq_len)
        sl = slice(gi * POOL_GROUP, (gi + 1) * POOL_GROUP)
        csg = cs[:, :, sl]
        cnt = (hi - lo).astype(jnp.float32)[None, :, None]
        groups.append((csg[:, hi] - csg[:, lo]) / cnt - hf[:, :, sl])
    pooled = jnp.stack(groups, axis=2).astype(h.dtype)
    y = jnp.einsum('blgc,gcd->blgd', pooled, w_pool).reshape(b, seq_len, d)
    return y * pool_scale


def _conv_ffn(h, w_up, conv_w, conv_b, w_down):
    u = h @ w_up
    val, gate = jnp.split(u, 2, axis=-1)
    gp = jnp.pad(gate, ((0, 0), (1, 1), (0, 0)))
    gate = gp[:, :-2] * conv_w[0] + gp[:, 1:-1] * conv_w[1] + gp[:, 2:] * conv_w[2] + conv_b
    return (jax.nn.gelu(gate, approximate=False) * val) @ w_down


def setup_inputs(seed: int = 0) -> dict:
    key = jax.random.key(seed)
    ks = jax.random.split(key, 24)
    f32 = jnp.float32

    def nrm(k, shape, scale):
        return jax.random.normal(k, shape, f32) * scale

    d = D_MODEL
    return {
        'x': nrm(ks[0], (BATCH, SEQ, d), 1.0),
        'c': nrm(ks[1], (BATCH, d), 1.0),
        'ctx': nrm(ks[2], (BATCH, CTX_LEN, d), 1.0),
        'c_ctx': nrm(ks[3], (d,), 1.0),
        'w_mod': nrm(ks[4], (DEPTH, d, 6 * d), 0.5 * d ** -0.5),
        'b_mod': nrm(ks[5], (DEPTH, 6 * d), 0.01),
        'norm1_g': 1.0 + nrm(ks[6], (DEPTH, d), 0.01),
        'norm2_g': 1.0 + nrm(ks[7], (DEPTH, d), 0.01),
        'w_in': nrm(ks[8], (N_EVEN, d, D_IN), d ** -0.5),
        'w_gate2': nrm(ks[9], (N_EVEN, 2, GLA_GATE_RANK, BQK_W), GLA_GATE_RANK ** -0.5),
        'b_gate': nrm(ks[10], (N_EVEN, 2, BQK_W), 0.1),
        'rpb': nrm(ks[11], (N_EVEN, NA_HEADS, 2 * NA_ROWS_MAX - 1, 2 * NA_COLS - 1), 0.1),
        'gla_norm_g': 1.0 + nrm(ks[12], (N_EVEN, GLA_HEADS, GLA_DV), 0.01),
        'w_out': nrm(ks[13], (N_EVEN, A_W + BV_W, d), (A_W + BV_W) ** -0.5),
        'pool_w': nrm(ks[14], (N_ODD, 4, POOL_GROUP, POOL_GROUP), POOL_GROUP ** -0.5),
        'pool_scale': 1.0 + nrm(ks[15], (N_ODD, d), 0.05),
        'w_up': nrm(ks[16], (DEPTH, d, 2 * D_FF), d ** -0.5),
        'conv_w': nrm(ks[17], (DEPTH, 3, D_FF), 3 ** -0.5),
        'conv_b': nrm(ks[18], (DEPTH, D_FF), 0.01),
        'w_down': nrm(ks[19], (DEPTH, D_FF, d), D_FF ** -0.5),
        'final_g': 1.0 + nrm(ks[20], (d,), 0.01),
    }


def reference(x, c, ctx, c_ctx, w_mod, b_mod, norm1_g, norm2_g, w_in, w_gate2, b_gate, rpb,
              gla_norm_g, w_out, pool_w, pool_scale, w_up, conv_w, conv_b, w_down, final_g):
    silu_c = jax.nn.silu(c)
    silu_cc = jax.nn.silu(c_ctx)
    x_lat = x
    x_ctx = ctx
    for i in range(DEPTH):
        is_even = i % 2 == 0
        need_ctx = any(j % 2 == 0 for j in range(i + 1, DEPTH))
        ctx_in = is_even or need_ctx
        sh1, sc1, g1, sh2, sc2, g2 = [m[:, None, :] for m in jnp.split(silu_c @ w_mod[i] + b_mod[i], 6, axis=-1)]
        h_lat = _modulate(_rmsnorm(x_lat, norm1_g[i]), sh1, sc1)
        if ctx_in:
            shc1, scc1, gc1, shc2, scc2, gc2 = jnp.split(silu_cc @ w_mod[i] + b_mod[i], 6, axis=-1)
            h_ctx = _modulate(_rmsnorm(x_ctx, norm1_g[i]), shc1, scc1)
        else:
            h_ctx = None
        if is_even:
            e = i // 2
            y_lat, y_ctx = _even_mixer(h_lat, h_ctx, w_in[e], w_gate2[e], b_gate[e], rpb[e],
                                       gla_norm_g[e], w_out[e], need_ctx)
        else:
            o = i // 2
            y_lat = _pool_mix(h_lat, pool_w[o], pool_scale[o])
            y_ctx = _pool_mix(h_ctx, pool_w[o], pool_scale[o]) if need_ctx else None
        x_lat = x_lat + g1 * y_lat
        x_lat = x_lat + g2 * _conv_ffn(_modulate(_rmsnorm(x_lat, norm2_g[i]), sh2, sc2),
                                       w_up[i], conv_w[i], conv_b[i], w_down[i])
        if need_ctx:
            x_ctx = x_ctx + gc1 * y_ctx
            x_ctx = x_ctx + gc2 * _conv_ffn(_modulate(_rmsnorm(x_ctx, norm2_g[i]), shc2, scc2),
                                            w_up[i], conv_w[i], conv_b[i], w_down[i])
    return _rmsnorm(x_lat, final_g)
```

```python
import functools

import numpy as np
import jax
import jax.numpy as jnp
from jax import lax
from jax.experimental import pallas as pl
from jax.experimental.pallas import tpu as pltpu

F32 = jnp.float32
BF16 = jnp.bfloat16

GRID_W = 64
HEAD_DIM = 128
NA_ROWS = 8
NA_COLS = 16
GLA_DK = 64
GLA_DV = 128
GLA_RANK = 16
GLA_TAU = 16.0
GLA_CHUNK = 64
POOL_WINDOWS = (2, 4, 8, 16)
ROPE_BASE = 10000.0
EPS = 1e-6
HALO = 16
VMEM_LIMIT = 56 * 1024 * 1024


def _cparams(n_axes):
    return pltpu.CompilerParams(dimension_semantics=("arbitrary",) * n_axes,
                                vmem_limit_bytes=VMEM_LIMIT)


def _bdot(a, b):
    return jnp.dot(a, b, preferred_element_type=F32)


def _bdot_nt(a, b):
    return lax.dot_general(a, b, (((1,), (1,)), ((), ())), preferred_element_type=F32)


def _norm_mod(x, g, shift, scale):
    ms = jnp.mean(x * x, axis=-1, keepdims=True)
    y = x * lax.rsqrt(ms + EPS) * g
    return y * (1.0 + scale) + shift


def _mod_kernel(c_ref, w_ref, b_ref, o_ref):
    c = c_ref[...]
    s = c / (1.0 + jnp.exp(-c))
    o_ref[0] = _bdot(s.astype(BF16), w_ref[0].astype(BF16)) + b_ref[0]


def _mod_call(cvec, w_mod, b_mod):
    depth, d, n = w_mod.shape
    tn = 1024
    return pl.pallas_call(
        _mod_kernel,
        out_shape=jax.ShapeDtypeStruct((depth, 8, n), F32),
        grid=(depth, n // tn),
        in_specs=[pl.BlockSpec((8, d), lambda l, j: (0, 0)),
                  pl.BlockSpec((1, d, tn), lambda l, j: (l, 0, j)),
                  pl.BlockSpec((1, 1, tn), lambda l, j: (l, 0, j))],
        out_specs=pl.BlockSpec((1, 8, tn), lambda l, j: (l, 0, j)),
        compiler_params=_cparams(2),
        name="mod_matvec",
    )(cvec, w_mod, b_mod.reshape(depth, 1, n))


def _proj_kernel(x_ref, g_ref, sh_ref, sc_ref, w_ref, o_ref, h_scr):
    @pl.when(pl.program_id(2) == 0)
    def _():
        h_scr[...] = _norm_mod(x_ref[0], g_ref[...], sh_ref[0], sc_ref[0]).astype(BF16)

    o_ref[0] = _bdot(h_scr[...], w_ref[...]).astype(o_ref.dtype)


def _proj_call(x, g, shift, scale, w, out_dtype, tm, tn):
    b, l, d = x.shape
    n = w.shape[1]
    return pl.pallas_call(
        _proj_kernel,
        out_shape=jax.ShapeDtypeStruct((b, l, n), out_dtype),
        grid=(b, l // tm, n // tn),
        in_specs=[pl.BlockSpec((1, tm, d), lambda bi, i, j: (bi, i, 0)),
                  pl.BlockSpec((1, d), lambda bi, i, j: (0, 0)),
                  pl.BlockSpec((1, 1, d), lambda bi, i, j: (bi, 0, 0)),
                  pl.BlockSpec((1, 1, d), lambda bi, i, j: (bi, 0, 0)),
                  pl.BlockSpec((d, tn), lambda bi, i, j: (0, j))],
        out_specs=pl.BlockSpec((1, tm, tn), lambda bi, i, j: (bi, i, j)),
        scratch_shapes=[pltpu.VMEM((tm, d), BF16)],
        compiler_params=_cparams(3),
        name="proj_in",
    )(x, g, shift, scale, w)


def _gate_kernel(x_ref, g_ref, sh_ref, sc_ref, wab_ref, wg2_ref, bg_ref, o_ref):
    h = _norm_mod(x_ref[0], g_ref[...], sh_ref[0], sc_ref[0]).astype(BF16)
    ab = _bdot(h, wab_ref[...])
    z = _bdot(ab.astype(BF16), wg2_ref[...]) + bg_ref[...]
    log_sig = jnp.minimum(z, 0.0) - jnp.log1p(jnp.exp(-jnp.abs(z)))
    o_ref[0] = log_sig * (1.0 / GLA_TAU)


def _gate_call(x, g, shift, scale, wab, wg2, bg, tm):
    b, l, d = x.shape
    n = wg2.shape[1]
    return pl.pallas_call(
        _gate_kernel,
        out_shape=jax.ShapeDtypeStruct((b, l, n), F32),
        grid=(b, l // tm),
        in_specs=[pl.BlockSpec((1, tm, d), lambda bi, i: (bi, i, 0)),
                  pl.BlockSpec((1, d), lambda bi, i: (0, 0)),
                  pl.BlockSpec((1, 1, d), lambda bi, i: (bi, 0, 0)),
                  pl.BlockSpec((1, 1, d), lambda bi, i: (bi, 0, 0)),
                  pl.BlockSpec(wab.shape, lambda bi, i: (0, 0)),
                  pl.BlockSpec(wg2.shape, lambda bi, i: (0, 0)),
                  pl.BlockSpec((1, n), lambda bi, i: (0, 0))],
        out_specs=pl.BlockSpec((1, tm, n), lambda bi, i: (bi, i, 0)),
        compiler_params=_cparams(2),
        name="gla_gates",
    )(x, g, shift, scale, wab, wg2, bg)


def _na_kernel(rpb_ref, q_ref, k_ref, v_ref, kc_ref, vc_ref, o_ref, tab_ref, *, rows, n_dr, n_dc):
    head = pl.program_id(1)
    scale = HEAD_DIM ** -0.5
    wr = NA_ROWS
    band = wr * GRID_W

    qc = lax.broadcasted_iota(jnp.int32, (GRID_W, 2 * GRID_W), 0)
    lane = lax.broadcasted_iota(jnp.int32, (GRID_W, 2 * GRID_W), 1)
    kc = lane & (GRID_W - 1)
    dc = jnp.clip(kc - qc + (NA_COLS - 1), 0, n_dc - 1)
    cstart = jnp.clip(qc - NA_COLS // 2, 0, GRID_W - NA_COLS)
    in_win = (kc >= cstart) & (kc < cstart + NA_COLS)
    left = lane < GRID_W
    base_off = head * (n_dr * n_dc)
    for dr in range(n_dr - 1):
        acc = jnp.zeros((GRID_W, 2 * GRID_W), F32)
        for c in range(n_dc):
            lo = rpb_ref[base_off + dr * n_dc + c]
            hi = rpb_ref[base_off + (dr + 1) * n_dc + c]
            acc = jnp.where(dc == c, jnp.where(left, lo, hi), acc)
        tab_ref[dr] = jnp.where(in_win, acc, -jnp.inf)

    kctx = kc_ref[0]
    vctx = vc_ref[0]

    def row_block(r, carry):
        rstart = jnp.clip(r - wr // 2, 0, rows - wr)
        q = q_ref[0, pl.ds(pl.multiple_of(r * GRID_W, GRID_W), GRID_W), :]
        kv_rows = pl.ds(pl.multiple_of(rstart * GRID_W, GRID_W), band)
        kb = k_ref[0, kv_rows, :]
        vb = v_ref[0, kv_rows, :]
        dr0 = rstart - r + (NA_ROWS - 1)
        bias = jnp.concatenate([tab_ref[dr0 + 2 * j] for j in range(wr // 2)], axis=1)
        s_loc = _bdot_nt(q, kb) * scale + bias
        s_ctx = _bdot_nt(q, kctx) * scale
        m = jnp.maximum(jnp.max(s_loc, axis=-1, keepdims=True), jnp.max(s_ctx, axis=-1, keepdims=True))
        p_loc = jnp.exp(s_loc - m)
        p_ctx = jnp.exp(s_ctx - m)
        denom = jnp.sum(p_loc, axis=-1, keepdims=True) + jnp.sum(p_ctx, axis=-1, keepdims=True)
        o = _bdot(p_loc.astype(BF16), vb) + _bdot(p_ctx.astype(BF16), vctx)
        o_ref[0, pl.ds(pl.multiple_of(r * GRID_W, GRID_W), GRID_W), :] = (o / denom).astype(o_ref.dtype)
        return carry

    lax.fori_loop(0, rows, row_block, 0)


def _na_call(pa, pa_ctx, rpb):
    b, l, _ = pa.shape
    lc = pa_ctx.shape[1]
    nh, n_dr, n_dc = rpb.shape
    rows = l // GRID_W
    kern = functools.partial(_na_kernel, rows=rows, n_dr=n_dr, n_dc=n_dc)
    return pl.pallas_call(
        kern,
        out_shape=jax.ShapeDtypeStruct((b, l, nh * HEAD_DIM), BF16),
        grid=(b, nh),
        in_specs=[pl.BlockSpec(memory_space=pltpu.SMEM),
                  pl.BlockSpec((1, l, HEAD_DIM), lambda bi, h: (bi, 0, h)),
                  pl.BlockSpec((1, l, HEAD_DIM), lambda bi, h: (bi, 0, nh + h)),
                  pl.BlockSpec((1, l, HEAD_DIM), lambda bi, h: (bi, 0, 2 * nh + h)),
                  pl.BlockSpec((1, lc, HEAD_DIM), lambda bi, h: (bi, 0, nh + h)),
                  pl.BlockSpec((1, lc, HEAD_DIM), lambda bi, h: (bi, 0, 2 * nh + h))],
        out_specs=pl.BlockSpec((1, l, HEAD_DIM), lambda bi, h: (bi, 0, h)),
        scratch_shapes=[pltpu.VMEM((n_dr - 1, GRID_W, 2 * GRID_W), F32)],
        compiler_params=_cparams(2),
        name="na_attention",
    )(rpb.reshape(-1), pa, pa, pa, pa_ctx, pa_ctx)


def _ctx_attn_kernel(q_ref, k_ref, v_ref, o_ref):
    q = q_ref[0]
    s = _bdot_nt(q, k_ref[0]) * (HEAD_DIM ** -0.5)
    m = jnp.max(s, axis=-1, keepdims=True)
    p = jnp.exp(s - m)
    denom = jnp.sum(p, axis=-1, keepdims=True)
    o_ref[0] = (_bdot(p.astype(BF16), v_ref[0]) / denom).astype(o_ref.dtype)


def _ctx_attn_call(pa_ctx, nh):
    b, lc, _ = pa_ctx.shape
    return pl.pallas_call(
        _ctx_attn_kernel,
        out_shape=jax.ShapeDtypeStruct((b, lc, nh * HEAD_DIM), BF16),
        grid=(b, nh),
        in_specs=[pl.BlockSpec((1, lc, HEAD_DIM), lambda bi, h: (bi, 0, h)),
                  pl.BlockSpec((1, lc, HEAD_DIM), lambda bi, h: (bi, 0, nh + h)),
                  pl.BlockSpec((1, lc, HEAD_DIM), lambda bi, h: (bi, 0, 2 * nh + h))],
        out_specs=pl.BlockSpec((1, lc, HEAD_DIM), lambda bi, h: (bi, 0, h)),
        compiler_params=_cparams(2),
        name="ctx_attention",
    )(pa_ctx, pa_ctx, pa_ctx)


_GLA_LEVELS = (32, 16, 8, 4, 2, 1)


def _gla_kernel(q_ref, k_ref, v_ref, g_ref, cos_ref, sin_ref, s0_ref, o_ref, sfin_ref, st_scr,
                *, rev, n_chunks, n_pairs):
    c = GLA_CHUNK
    blk_i = pl.program_id(1)

    @pl.when(blk_i == 0)
    def _():
        st_scr[...] = s0_ref[0]

    ii = lax.broadcasted_iota(jnp.int32, (c, c), 0)
    jj = lax.broadcasted_iota(jnp.int32, (c, c), 1)

    def ref_row(s):
        mid = (ii // (2 * s)) * (2 * s) + s
        return mid - 1 if rev else mid

    def incl(row):
        return (jj >= row) if rev else (jj <= row)

    big = jnp.concatenate([incl(ii).astype(F32)] + [incl(ref_row(s)).astype(F32) for s in _GLA_LEVELS], axis=0)

    i2 = lax.broadcasted_iota(jnp.int32, (c, 2 * c), 0)
    l2 = lax.broadcasted_iota(jnp.int32, (c, 2 * c), 1)
    j2 = l2 & (c - 1)
    masks = [i2 == j2]
    for s in _GLA_LEVELS:
        same = (i2 // (2 * s)) == (j2 // (2 * s))
        qi = i2 % (2 * s)
        kj = j2 % (2 * s)
        if rev:
            masks.append(same & (qi < s) & (kj >= s))
        else:
            masks.append(same & (qi >= s) & (kj < s))
    left128 = l2 < c
    lane32 = l2 & 31
    lv = lax.broadcasted_iota(jnp.int32, (c, 2 * GLA_DV), 1)
    left256 = lv < GLA_DV
    sr = lax.broadcasted_iota(jnp.int32, (2 * GLA_DV, 2 * GLA_DK), 0)
    sc_ = lax.broadcasted_iota(jnp.int32, (2 * GLA_DV, 2 * GLA_DK), 1)
    st_diag = (sr // GLA_DV) == (sc_ // GLA_DK)

    def rope(x, cos, sin):
        outs = []
        for p in range(n_pairs):
            xs = x[:, p * 128:(p + 1) * 128]
            up = pltpu.roll(xs, 16, axis=1)
            dn = pltpu.roll(xs, 112, axis=1)
            sw = jnp.where(lane32 < 16, dn, up)
            outs.append(xs * cos + sw * sin)
        return jnp.concatenate(outs, axis=1)

    end_row = 0 if rev else c - 1

    def chunk(ci, carry):
        cidx = (n_chunks - 1 - ci) if rev else ci
        rows = pl.ds(pl.multiple_of(cidx * c, c), c)
        cos = cos_ref[rows, :]
        sin = sin_ref[rows, :]
        q = rope(q_ref[0, rows, :], cos, sin) * (GLA_DK ** -0.5)
        k = rope(k_ref[0, rows, :], cos, sin)
        v = v_ref[0, rows, :]
        g = g_ref[0, rows, :]
        sums = jnp.dot(big, g, preferred_element_type=F32, precision=lax.Precision.HIGHEST)
        cum = sums[0:c]
        last = cum[end_row:end_row + 1, :]
        q_in = (q * jnp.exp(cum)).astype(BF16)
        k_out = (k * jnp.exp(last - cum)).astype(BF16)
        e_last = jnp.exp(last)

        qs = [q.astype(BF16)]
        ks = [k.astype(BF16)]
        for li in range(len(_GLA_LEVELS)):
            dq = cum - sums[(li + 1) * c:(li + 2) * c]
            qs.append((q * jnp.exp(jnp.minimum(dq, 0.0))).astype(BF16))
            ks.append((k * jnp.exp(jnp.minimum(-dq, 0.0))).astype(BF16))

        outs = []
        for p in range(n_pairs):
            ksl = slice(p * 128, (p + 1) * 128)
            vsl = slice(p * 256, (p + 1) * 256)
            attn = jnp.zeros((c, 2 * c), F32)
            for lv_i in range(len(masks)):
                kp = ks[lv_i][:, ksl]
                kbd = jnp.concatenate([jnp.where(left128, kp, 0), jnp.where(left128, 0, kp)], axis=0)
                a = _bdot_nt(qs[lv_i][:, ksl], kbd)
                attn = jnp.where(masks[lv_i], a, attn)
            vp = v[:, vsl]
            vp16 = vp.astype(BF16)
            vbd = jnp.concatenate([jnp.where(left256, vp16, 0), jnp.where(left256, 0, vp16)], axis=0)
            st = st_scr[p]
            o_p = _bdot(attn.astype(BF16), vbd) + _bdot_nt(q_in[:, ksl], st.astype(BF16))
            outs.append(o_p)
            upd = lax.dot_general(vp16, k_out[:, ksl], (((0,), (0,)), ((), ())), preferred_element_type=F32)
            st_scr[p] = st * e_last[:, ksl] + jnp.where(st_diag, upd, 0.0)
        o_ref[0, rows, :] = jnp.concatenate(outs, axis=1)
        return carry

    lax.fori_loop(0, n_chunks, chunk, 0)

    @pl.when(blk_i == pl.num_programs(1) - 1)
    def _():
        sfin_ref[0] = st_scr[...]


def _gla_call(pb, lg, cos, sin, s0, *, rev, tb):
    b, l, _ = pb.shape
    nqk = lg.shape[2] // 2
    n_pairs = nqk // 128
    nv = n_pairs * 2 * GLA_DV
    nb = l // tb
    n_chunks = tb // GLA_CHUNK
    if rev:
        tok = lambda bi, i: nb - 1 - i
    else:
        tok = lambda bi, i: i
    kern = functools.partial(_gla_kernel, rev=rev, n_chunks=n_chunks, n_pairs=n_pairs)
    st_shape = (n_pairs, 2 * GLA_DV, 2 * GLA_DK)
    return pl.pallas_call(
        kern,
        out_shape=(jax.ShapeDtypeStruct((b, l, nv), F32),
                   jax.ShapeDtypeStruct((b,) + st_shape, F32)),
        grid=(b, nb),
        in_specs=[pl.BlockSpec((1, tb, nqk), lambda bi, i: (bi, tok(bi, i), 0)),
                  pl.BlockSpec((1, tb, nqk), lambda bi, i: (bi, tok(bi, i), 1)),
                  pl.BlockSpec((1, tb, nv), lambda bi, i: (bi, tok(bi, i), 1)),
                  pl.BlockSpec((1, tb, nqk), lambda bi, i: (bi, tok(bi, i), 1 if rev else 0)),
                  pl.BlockSpec((tb, 128), lambda bi, i: (tok(bi, i), 0)),
                  pl.BlockSpec((tb, 128), lambda bi, i: (tok(bi, i), 0)),
                  pl.BlockSpec((1,) + st_shape, lambda bi, i: (bi, 0, 0, 0))],
        out_specs=(pl.BlockSpec((1, tb, nv), lambda bi, i: (bi, tok(bi, i), 0)),
                   pl.BlockSpec((1,) + st_shape, lambda bi, i: (bi, 0, 0, 0))),
        scratch_shapes=[pltpu.VMEM(st_shape, F32)],
        compiler_params=_cparams(2),
        name="gla_rev" if rev else "gla_fwd",
    )(pb, pb, pb, lg, cos, sin, s0)


def _rope_tables(seq_len):
    t = jnp.arange(seq_len)
    row = (t // GRID_W).astype(F32)
    col = (t % GRID_W).astype(F32)
    nf = GLA_DK // 4
    inv = ROPE_BASE ** (-jnp.arange(nf, dtype=F32) / nf)
    ar = row[:, None] * inv[None, :]
    ac = col[:, None] * inv[None, :]
    cos = jnp.concatenate([jnp.cos(ar), jnp.cos(ar), jnp.cos(ac), jnp.cos(ac)], axis=1)
    sin = jnp.concatenate([-jnp.sin(ar), jnp.sin(ar), -jnp.sin(ac), jnp.sin(ac)], axis=1)
    return jnp.tile(cos, (1, 2)), jnp.tile(sin, (1, 2))


def _combine_kernel(oa_ref, of_ref, ob_ref, gb_ref, gg_ref, w_ref, x_ref, g1_ref, o_ref, *, n_heads):
    ob = of_ref[0] + ob_ref[0]
    parts = []
    for h in range(n_heads):
        oh = ob[:, h * GLA_DV:(h + 1) * GLA_DV]
        ms = jnp.mean(oh * oh, axis=-1, keepdims=True)
        parts.append(oh * lax.rsqrt(ms + EPS))
    obn = jnp.concatenate(parts, axis=1) * gg_ref[...]
    gb = gb_ref[0]
    yb = obn * (gb / (1.0 + jnp.exp(-gb)))
    na_w = oa_ref.shape[2]
    y = _bdot(oa_ref[0], w_ref[0:na_w, :]) + _bdot(yb.astype(BF16), w_ref[na_w:, :])
    o_ref[0] = x_ref[0] + g1_ref[0] * y


def _combine_call(oa, of, ob, pb, gla_g, w_out, x, g1, tm):
    b, l, d = x.shape
    na_w = oa.shape[2]
    gl_w = of.shape[2]
    n_heads = gl_w // GLA_DV
    gate_blk = (pb.shape[2] - gl_w) // gl_w
    kern = functools.partial(_combine_kernel, n_heads=n_heads)
    return pl.pallas_call(
        kern,
        out_shape=jax.ShapeDtypeStruct((b, l, d), F32),
        grid=(b, l // tm),
        in_specs=[pl.BlockSpec((1, tm, na_w), lambda bi, i: (bi, i, 0)),
                  pl.BlockSpec((1, tm, gl_w), lambda bi, i: (bi, i, 0)),
                  pl.BlockSpec((1, tm, gl_w), lambda bi, i: (bi, i, 0)),
                  pl.BlockSpec((1, tm, gl_w), lambda bi, i: (bi, i, gate_blk)),
                  pl.BlockSpec((1, gl_w), lambda bi, i: (0, 0)),
                  pl.BlockSpec(w_out.shape, lambda bi, i: (0, 0)),
                  pl.BlockSpec((1, tm, d), lambda bi, i: (bi, i, 0)),
                  pl.BlockSpec((1, 1, d), lambda bi, i: (bi, 0, 0))],
        out_specs=pl.BlockSpec((1, tm, d), lambda bi, i: (bi, i, 0)),
        compiler_params=_cparams(2),
        name="combine_out",
    )(oa, of, ob, pb, gla_g, w_out, x, g1)


def _pool_kernel(x_ref, xp_ref, xn_ref, g_ref, sh_ref, sc_ref, w_ref, ps_ref, g1_ref, o_ref, h_scr,
                 *, tm, seq_len):
    i = pl.program_id(1)
    n_i = pl.num_programs(1)
    g = g_ref[...]
    sh = sh_ref[0]
    sc = sc_ref[0]
    x = x_ref[0]
    hm = _norm_mod(x, g, sh, sc)
    h_scr[HALO:HALO + tm, :] = hm
    h_scr[0:HALO, :] = jnp.where(i > 0, _norm_mod(xp_ref[0], g, sh, sc), 0.0)
    h_scr[HALO + tm:, :] = jnp.where(i < n_i - 1, _norm_mod(xn_ref[0], g, sh, sc), 0.0)

    t = i * tm + lax.broadcasted_iota(jnp.int32, (tm, 1), 0)
    grp = hm.shape[1] // len(POOL_WINDOWS)
    ys = []
    for gi, w in enumerate(POOL_WINDOWS):
        cols = slice(gi * grp, (gi + 1) * grp)
        acc = None
        for off in range(-(w // 2), w // 2):
            term = h_scr[HALO + off:HALO + off + tm, cols]
            acc = term if acc is None else acc + term
        cnt = (jnp.minimum(t + w // 2, seq_len) - jnp.maximum(t - w // 2, 0)).astype(F32)
        pooled = acc / cnt - hm[:, cols]
        ys.append(_bdot(pooled.astype(BF16), w_ref[gi]))
    y = jnp.concatenate(ys, axis=1) * ps_ref[...]
    o_ref[0] = x + g1_ref[0] * y


def _halo_specs(tm, l, d):
    per = tm // HALO
    last = l // HALO - 1
    prev = pl.BlockSpec((1, HALO, d), lambda bi, i, *_: (bi, jnp.maximum(i * per - 1, 0), 0))
    nxt = pl.BlockSpec((1, HALO, d), lambda bi, i, *_: (bi, jnp.minimum((i + 1) * per, last), 0))
    return prev, nxt


def _pool_call(x, g, shift, scale, pool_w, pool_scale, g1, tm):
    b, l, d = x.shape
    prev, nxt = _halo_specs(tm, l, d)
    kern = functools.partial(_pool_kernel, tm=tm, seq_len=l)
    return pl.pallas_call(
        kern,
        out_shape=jax.ShapeDtypeStruct((b, l, d), F32),
        grid=(b, l // tm),
        in_specs=[pl.BlockSpec((1, tm, d), lambda bi, i: (bi, i, 0)), prev, nxt,
                  pl.BlockSpec((1, d), lambda bi, i: (0, 0)),
                  pl.BlockSpec((1, 1, d), lambda bi, i: (bi, 0, 0)),
                  pl.BlockSpec((1, 1, d), lambda bi, i: (bi, 0, 0)),
                  pl.BlockSpec(pool_w.shape, lambda bi, i: (0, 0, 0)),
                  pl.BlockSpec((1, d), lambda bi, i: (0, 0)),
                  pl.BlockSpec((1, 1, d), lambda bi, i: (bi, 0, 0))],
        out_specs=pl.BlockSpec((1, tm, d), lambda bi, i: (bi, i, 0)),
        scratch_shapes=[pltpu.VMEM((tm + 2 * HALO, d), F32)],
        compiler_params=_cparams(2),
        name="pool_mixer",
    )(x, x, x, g, shift, scale, pool_w, pool_scale, g1)


def _ffn_kernel(x_ref, xp_ref, xn_ref, g_ref, sh_ref, sc_ref, wv_ref, wg_ref, cw_ref, cb_ref, wd_ref,
                g2_ref, o_ref, h_scr, acc_scr, *, tm):
    i = pl.program_id(1)
    j = pl.program_id(2)
    n_i = pl.num_programs(1)

    @pl.when(j == 0)
    def _():
        g = g_ref[...]
        sh = sh_ref[0]
        sc = sc_ref[0]
        h_scr[HALO:HALO + tm, :] = _norm_mod(x_ref[0], g, sh, sc).astype(BF16)
        h_scr[0:HALO, :] = jnp.where(i > 0, _norm_mod(xp_ref[0], g, sh, sc), 0.0).astype(BF16)
        h_scr[HALO + tm:, :] = jnp.where(i < n_i - 1, _norm_mod(xn_ref[0], g, sh, sc), 0.0).astype(BF16)
        acc_scr[...] = jnp.zeros_like(acc_scr)

    rows_all = tm + 2 * HALO
    ug = _bdot(h_scr[...], wg_ref[...])
    uv = _bdot(h_scr[HALO:HALO + tm, :], wv_ref[...])
    cw = cw_ref[...]
    g_prev = pltpu.roll(ug, 1, axis=0)[HALO:HALO + tm]
    g_next = pltpu.roll(ug, rows_all - 1, axis=0)[HALO:HALO + tm]
    gate = g_prev * cw[0:1] + ug[HALO:HALO + tm] * cw[1:2] + g_next * cw[2:3] + cb_ref[...]
    act = 0.5 * gate * (1.0 + lax.erf(gate * (2.0 ** -0.5))) * uv
    acc_scr[...] += _bdot(act.astype(BF16), wd_ref[...])

    @pl.when(j == pl.num_programs(2) - 1)
    def _():
        o_ref[0] = x_ref[0] + g2_ref[0] * acc_scr[...]


def _ffn_call(x, g, shift, scale, w_up, conv_w, conv_b, w_down, g2, tm, tf):
    b, l, d = x.shape
    f = w_down.shape[0]
    nf = f // tf
    prev, nxt = _halo_specs(tm, l, d)
    kern = functools.partial(_ffn_kernel, tm=tm)
    return pl.pallas_call(
        kern,
        out_shape=jax.ShapeDtypeStruct((b, l, d), F32),
        grid=(b, l // tm, nf),
        in_specs=[pl.BlockSpec((1, tm, d), lambda bi, i, j: (bi, i, 0)), prev, nxt,
                  pl.BlockSpec((1, d), lambda bi, i, j: (0, 0)),
                  pl.BlockSpec((1, 1, d), lambda bi, i, j: (bi, 0, 0)),
                  pl.BlockSpec((1, 1, d), lambda bi, i, j: (bi, 0, 0)),
                  pl.BlockSpec((d, tf), lambda bi, i, j: (0, j)),
                  pl.BlockSpec((d, tf), lambda bi, i, j: (0, nf + j)),
                  pl.BlockSpec((3, tf), lambda bi, i, j: (0, j)),
                  pl.BlockSpec((1, tf), lambda bi, i, j: (0, j)),
                  pl.BlockSpec((tf, d), lambda bi, i, j: (j, 0)),
                  pl.BlockSpec((1, 1, d), lambda bi, i, j: (bi, 0, 0))],
        out_specs=pl.BlockSpec((1, tm, d), lambda bi, i, j: (bi, i, 0)),
        scratch_shapes=[pltpu.VMEM((tm + 2 * HALO, d), BF16), pltpu.VMEM((tm, d), F32)],
        compiler_params=_cparams(3),
        name="conv_ffn",
    )(x, x, x, g, shift, scale, w_up, w_up, conv_w, conv_b, w_down, g2)


def _final_kernel(x_ref, g_ref, o_ref):
    x = x_ref[0]
    ms = jnp.mean(x * x, axis=-1, keepdims=True)
    o_ref[0] = x * lax.rsqrt(ms + EPS) * g_ref[...]


def _final_call(x, g, tm):
    b, l, d = x.shape
    return pl.pallas_call(
        _final_kernel,
        out_shape=jax.ShapeDtypeStruct((b, l, d), F32),
        grid=(b, l // tm),
        in_specs=[pl.BlockSpec((1, tm, d), lambda bi, i: (bi, i, 0)),
                  pl.BlockSpec((1, d), lambda bi, i: (0, 0))],
        out_specs=pl.BlockSpec((1, tm, d), lambda bi, i: (bi, i, 0)),
        compiler_params=_cparams(2),
        name="final_norm",
    )(x, g)


def _tile(l, want):
    return min(l, want)


def _even_mixer(x_lat, x_ctx, mods_lat, mods_ctx, n1, w_in, w_gate2, b_gate, rpb, gla_g, w_out, need_ctx,
                rope_lat, rope_ctx):
    sh, sc, g1 = mods_lat
    shc, scc, gc1 = mods_ctx
    nh = rpb.shape[0]
    a_w = 3 * nh * HEAD_DIM
    n_gla = w_gate2.shape[2] // GLA_DK
    b_w = 2 * n_gla * GLA_DK + 2 * n_gla * GLA_DV
    w_a = w_in[:, :a_w].astype(BF16)
    w_b = w_in[:, a_w:a_w + b_w].astype(BF16)
    w_ab = jnp.pad(w_in[:, a_w + b_w:], ((0, 0), (0, 128 - 2 * GLA_RANK))).astype(BF16)
    nqk = n_gla * GLA_DK
    wg2 = jnp.zeros((128, 2 * nqk), F32)
    wg2 = wg2.at[0:GLA_RANK, 0:nqk].set(w_gate2[0]).at[GLA_RANK:2 * GLA_RANK, nqk:].set(w_gate2[1]).astype(BF16)
    bg = b_gate.reshape(1, 2 * nqk)

    l = x_lat.shape[1]
    lc = x_ctx.shape[1]
    tm, tmc = _tile(l, 1024), _tile(lc, 1024)
    pa = _proj_call(x_lat, n1, sh, sc, w_a, BF16, tm, 1024)
    pb = _proj_call(x_lat, n1, sh, sc, w_b, F32, tm, 1024)
    lg = _gate_call(x_lat, n1, sh, sc, w_ab, wg2, bg, _tile(l, 512))
    pa_c = _proj_call(x_ctx, n1, shc, scc, w_a, BF16, tmc, 1024)
    pb_c = _proj_call(x_ctx, n1, shc, scc, w_b, F32, tmc, 1024)
    lg_c = _gate_call(x_ctx, n1, shc, scc, w_ab, wg2, bg, tmc)

    oa = _na_call(pa, pa_c, rpb)

    bsz = x_lat.shape[0]
    s0 = jnp.zeros((bsz, nqk // 128, 2 * GLA_DV, 2 * GLA_DK), F32)
    of_c, s_f = _gla_call(pb_c, lg_c, rope_ctx[0], rope_ctx[1], s0, rev=False, tb=_tile(lc, 512))
    ob_c, s_b = _gla_call(pb_c, lg_c, rope_ctx[0], rope_ctx[1], s0, rev=True, tb=_tile(lc, 512))
    of, _ = _gla_call(pb, lg, rope_lat[0], rope_lat[1], s_f, rev=False, tb=_tile(l, 512))
    ob, _ = _gla_call(pb, lg, rope_lat[0], rope_lat[1], s_b, rev=True, tb=_tile(l, 512))

    w_o = w_out.astype(BF16)
    gg = gla_g.reshape(1, -1)
    x_lat = _combine_call(oa, of, ob, pb, gg, w_o, x_lat, g1, _tile(l, 512))
    if need_ctx:
        oa_c = _ctx_attn_call(pa_c, nh)
        x_ctx = _combine_call(oa_c, of_c, ob_c, pb_c, gg, w_o, x_ctx, gc1, _tile(lc, 512))
    return x_lat, x_ctx


def kernel(x, c, ctx, c_ctx, w_mod, b_mod, norm1_g, norm2_g, w_in, w_gate2, b_gate, rpb, gla_norm_g, w_out,
           pool_w, pool_scale, w_up, conv_w, conv_b, w_down, final_g):
    bsz, seq, d = x.shape
    lc = ctx.shape[1]
    depth = w_mod.shape[0]

    cvec = jnp.zeros((8, d), F32).at[0:bsz].set(c).at[bsz].set(c_ctx)
    mods = _mod_call(cvec, w_mod, b_mod)

    rope_lat = _rope_tables(seq)
    rope_ctx = (jnp.ones((lc, 128), F32), jnp.zeros((lc, 128), F32))

    x_lat, x_ctx = x, ctx
    for i in range(depth):
        is_even = i % 2 == 0
        need_ctx = any(j % 2 == 0 for j in range(i + 1, depth))
        m = mods[i].reshape(8, 6, d)
        lat = [m[0:bsz, k][:, None, :] for k in range(6)]
        cx = [jnp.broadcast_to(m[bsz, k][None, None, :], (bsz, 1, d)) for k in range(6)]
        n1 = norm1_g[i].reshape(1, d)
        n2 = norm2_g[i].reshape(1, d)
        wu = w_up[i].astype(BF16)
        wd = w_down[i].astype(BF16)
        cb = conv_b[i].reshape(1, -1)
        if is_even:
            e = i // 2
            x_lat, x_ctx = _even_mixer(x_lat, x_ctx, lat[0:3], cx[0:3], n1, w_in[e], w_gate2[e], b_gate[e], rpb[e],
                                       gla_norm_g[e], w_out[e], need_ctx, rope_lat, rope_ctx)
        else:
            o = i // 2
            pw = pool_w[o].astype(BF16)
            ps = pool_scale[o].reshape(1, d)
            x_lat = _pool_call(x_lat, n1, lat[0], lat[1], pw, ps, lat[2], _tile(seq, 512))
            if need_ctx:
                x_ctx = _pool_call(x_ctx, n1, cx[0], cx[1], pw, ps, cx[2], _tile(lc, 512))
        x_lat = _ffn_call(x_lat, n2, lat[3], lat[4], wu, conv_w[i], cb, wd, lat[5], _tile(seq, 512), 512)
        if need_ctx:
            x_ctx = _ffn_call(x_ctx, n2, cx[3], cx[4], wu, conv_w[i], cb, wd, cx[5], _tile(lc, 512), 512)
    return _final_call(x_lat, final_g.reshape(1, d), _tile(seq, 512))
```

```python
import functools

import numpy as np
import jax
import jax.numpy as jnp
from jax import lax
from jax.experimental import pallas as pl
from jax.experimental.pallas import tpu as pltpu

F32 = jnp.float32
BF16 = jnp.bfloat16

GRID_W = 64
HEAD_DIM = 128
NA_ROWS = 8
NA_COLS = 16
NA_GROUP = 4
GLA_DK = 64
GLA_DV = 128
GLA_RANK = 16
GLA_TAU = 16.0
GLA_CHUNK = 64
POOL_WINDOWS = (2, 4, 8, 16)
ROPE_BASE = 10000.0
EPS = 1e-6
HALO = 16
VMEM_LIMIT = 56 * 1024 * 1024


def _cparams(n_axes):
    return pltpu.CompilerParams(dimension_semantics=("arbitrary",) * n_axes,
                                vmem_limit_bytes=VMEM_LIMIT)


def _bdot(a, b):
    return jnp.dot(a, b, preferred_element_type=F32)


def _bdot_nt(a, b):
    return lax.dot_general(a, b, (((1,), (1,)), ((), ())), preferred_element_type=F32)


def _norm_mod(x, g, shift, scale):
    ms = jnp.mean(x * x, axis=-1, keepdims=True)
    y = x * lax.rsqrt(ms + EPS) * g
    return y * (1.0 + scale) + shift


def _mod_kernel(c_ref, w_ref, b_ref, o_ref):
    c = c_ref[...]
    s = c / (1.0 + jnp.exp(-c))
    o_ref[0] = _bdot(s.astype(BF16), w_ref[0].astype(BF16)) + b_ref[0]


def _mod_call(cvec, w_mod, b_mod):
    depth, d, n = w_mod.shape
    tn = 1024
    return pl.pallas_call(
        _mod_kernel,
        out_shape=jax.ShapeDtypeStruct((depth, 8, n), F32),
        grid=(depth, n // tn),
        in_specs=[pl.BlockSpec((8, d), lambda l, j: (0, 0)),
                  pl.BlockSpec((1, d, tn), lambda l, j: (l, 0, j)),
                  pl.BlockSpec((1, 1, tn), lambda l, j: (l, 0, j))],
        out_specs=pl.BlockSpec((1, 8, tn), lambda l, j: (l, 0, j)),
        compiler_params=_cparams(2),
        name="mod_matvec",
    )(cvec, w_mod, b_mod.reshape(depth, 1, n))


def _proj_kernel(x_ref, g_ref, sh_ref, sc_ref, wa_ref, wb_ref, wab_ref, wg2_ref, bg_ref,
                 oa_ref, ob_ref, lg_ref, h_scr, *, na_tiles):
    j = pl.program_id(2)

    @pl.when(j == 0)
    def _():
        h = _norm_mod(x_ref[0], g_ref[...], sh_ref[0], sc_ref[0]).astype(BF16)
        h_scr[...] = h
        ab = _bdot(h, wab_ref[...])
        z = _bdot(ab.astype(BF16), wg2_ref[...]) + bg_ref[...]
        log_sig = jnp.minimum(z, 0.0) - jnp.log1p(jnp.exp(-jnp.abs(z)))
        lg_ref[0] = log_sig * (1.0 / GLA_TAU)

    @pl.when(j < na_tiles)
    def _():
        oa_ref[0] = _bdot(h_scr[...], wa_ref[...]).astype(oa_ref.dtype)

    @pl.when(j >= na_tiles)
    def _():
        ob_ref[0] = _bdot(h_scr[...], wb_ref[...])


def _proj_call(x, g, shift, scale, w_a, w_b, wab, wg2, bg, tm, tn):
    b, l, d = x.shape
    na, nb_ = w_a.shape[1], w_b.shape[1]
    ta, tb_ = na // tn, nb_ // tn
    ng = wg2.shape[1]
    kern = functools.partial(_proj_kernel, na_tiles=ta)
    return pl.pallas_call(
        kern,
        out_shape=(jax.ShapeDtypeStruct((b, l, na), BF16), jax.ShapeDtypeStruct((b, l, nb_), F32),
                   jax.ShapeDtypeStruct((b, l, ng), F32)),
        grid=(b, l // tm, ta + tb_),
        in_specs=[pl.BlockSpec((1, tm, d), lambda bi, i, j: (bi, i, 0)),
                  pl.BlockSpec((1, d), lambda bi, i, j: (0, 0)),
                  pl.BlockSpec((1, 1, d), lambda bi, i, j: (bi, 0, 0)),
                  pl.BlockSpec((1, 1, d), lambda bi, i, j: (bi, 0, 0)),
                  pl.BlockSpec((d, tn), lambda bi, i, j: (0, jnp.minimum(j, ta - 1))),
                  pl.BlockSpec((d, tn), lambda bi, i, j: (0, jnp.maximum(j - ta, 0))),
                  pl.BlockSpec(wab.shape, lambda bi, i, j: (0, 0)),
                  pl.BlockSpec(wg2.shape, lambda bi, i, j: (0, 0)),
                  pl.BlockSpec((1, ng), lambda bi, i, j: (0, 0))],
        out_specs=(pl.BlockSpec((1, tm, tn), lambda bi, i, j: (bi, i, jnp.minimum(j, ta - 1))),
                   pl.BlockSpec((1, tm, tn), lambda bi, i, j: (bi, i, jnp.maximum(j - ta, 0))),
                   pl.BlockSpec((1, tm, ng), lambda bi, i, j: (bi, i, 0))),
        scratch_shapes=[pltpu.VMEM((tm, d), BF16)],
        compiler_params=_cparams(3),
        name="proj_in",
    )(x, g, shift, scale, w_a, w_b, wab, wg2, bg)


def _na_kernel(rpb_ref, q_ref, k_ref, v_ref, kc_ref, vc_ref, o_ref, tab_ref, *, rows, n_dr, n_dc):
    head = pl.program_id(1)
    scale = HEAD_DIM ** -0.5
    wr = NA_ROWS

    qc = lax.broadcasted_iota(jnp.int32, (GRID_W, 2 * GRID_W), 0)
    lane = lax.broadcasted_iota(jnp.int32, (GRID_W, 2 * GRID_W), 1)
    kc = lane & (GRID_W - 1)
    dc = jnp.clip(kc - qc + (NA_COLS - 1), 0, n_dc - 1)
    cstart = jnp.clip(qc - NA_COLS // 2, 0, GRID_W - NA_COLS)
    in_win = (kc >= cstart) & (kc < cstart + NA_COLS)
    left = lane < GRID_W
    base_off = head * (n_dr * n_dc)
    for d in range(-1, n_dr):
        acc = jnp.zeros((GRID_W, 2 * GRID_W), F32)
        for c in range(n_dc):
            if d < 0:
                val = rpb_ref[base_off + (d + 1) * n_dc + c]
            elif d + 1 >= n_dr:
                val = rpb_ref[base_off + d * n_dc + c]
            else:
                val = jnp.where(left, rpb_ref[base_off + d * n_dc + c], rpb_ref[base_off + (d + 1) * n_dc + c])
            acc = jnp.where(dc == c, val, acc)
        keep = in_win & ~left if d < 0 else (in_win & left if d + 1 >= n_dr else in_win)
        tab_ref[d + 1] = jnp.where(keep, acc, -jnp.inf)

    kctx = kc_ref[0]
    vctx = vc_ref[0]
    grp = NA_GROUP
    union = grp + wr
    neg_inf = jnp.full((GRID_W, 2 * GRID_W), -jnp.inf, F32)

    def row_group(gi, carry):
        r0 = gi * grp
        ustart = jnp.clip(r0 - wr // 2, 0, rows - union)
        q_rows = pl.ds(pl.multiple_of(r0 * GRID_W, grp * GRID_W), grp * GRID_W)
        kv_rows = pl.ds(pl.multiple_of(ustart * GRID_W, GRID_W), union * GRID_W)
        q = q_ref[0, q_rows, :]
        kb = k_ref[0, kv_rows, :]
        vb = v_ref[0, kv_rows, :]
        biases = []
        for g in range(grp):
            rq = r0 + g
            rstart = jnp.clip(rq - wr // 2, 0, rows - wr)
            tiles = []
            for j in range(union // 2):
                ku = ustart + 2 * j
                d = ku - rq + (NA_ROWS - 1)
                ok_l = ((ku >= rstart) & (ku < rstart + wr)).astype(jnp.int32)
                ok_r = ((ku + 1 >= rstart) & (ku + 1 < rstart + wr)).astype(jnp.int32)
                tile = tab_ref[jnp.clip(d, -1, n_dr - 1) + 1]
                tiles.append(jnp.where(jnp.where(left, ok_l, ok_r) > 0, tile, neg_inf))
            biases.append(jnp.concatenate(tiles, axis=1))
        s_loc = _bdot_nt(q, kb) * scale + jnp.concatenate(biases, axis=0)
        s_ctx = _bdot_nt(q, kctx) * scale
        m = jnp.maximum(jnp.max(s_loc, axis=-1, keepdims=True), jnp.max(s_ctx, axis=-1, keepdims=True))
        p_loc = jnp.exp(s_loc - m)
        p_ctx = jnp.exp(s_ctx - m)
        denom = jnp.sum(p_loc, axis=-1, keepdims=True) + jnp.sum(p_ctx, axis=-1, keepdims=True)
        o = _bdot(p_loc.astype(BF16), vb) + _bdot(p_ctx.astype(BF16), vctx)
        o_ref[0, q_rows, :] = (o / denom).astype(o_ref.dtype)
        return carry

    lax.fori_loop(0, rows // grp, row_group, 0, unroll=2)


def _na_call(pa, pa_ctx, rpb):
    b, l, _ = pa.shape
    lc = pa_ctx.shape[1]
    nh, n_dr, n_dc = rpb.shape
    rows = l // GRID_W
    kern = functools.partial(_na_kernel, rows=rows, n_dr=n_dr, n_dc=n_dc)
    return pl.pallas_call(
        kern,
        out_shape=jax.ShapeDtypeStruct((b, l, nh * HEAD_DIM), BF16),
        grid=(b, nh),
        in_specs=[pl.BlockSpec(memory_space=pltpu.SMEM),
                  pl.BlockSpec((1, l, HEAD_DIM), lambda bi, h: (bi, 0, h)),
                  pl.BlockSpec((1, l, HEAD_DIM), lambda bi, h: (bi, 0, nh + h)),
                  pl.BlockSpec((1, l, HEAD_DIM), lambda bi, h: (bi, 0, 2 * nh + h)),
                  pl.BlockSpec((1, lc, HEAD_DIM), lambda bi, h: (bi, 0, nh + h)),
                  pl.BlockSpec((1, lc, HEAD_DIM), lambda bi, h: (bi, 0, 2 * nh + h))],
        out_specs=pl.BlockSpec((1, l, HEAD_DIM), lambda bi, h: (bi, 0, h)),
        scratch_shapes=[pltpu.VMEM((n_dr + 1, GRID_W, 2 * GRID_W), F32)],
        compiler_params=_cparams(2),
        name="na_attention",
    )(rpb.reshape(-1), pa, pa, pa, pa_ctx, pa_ctx)


def _ctx_attn_kernel(q_ref, k_ref, v_ref, o_ref):
    q = q_ref[0]
    s = _bdot_nt(q, k_ref[0]) * (HEAD_DIM ** -0.5)
    m = jnp.max(s, axis=-1, keepdims=True)
    p = jnp.exp(s - m)
    denom = jnp.sum(p, axis=-1, keepdims=True)
    o_ref[0] = (_bdot(p.astype(BF16), v_ref[0]) / denom).astype(o_ref.dtype)


def _ctx_attn_call(pa_ctx, nh):
    b, lc, _ = pa_ctx.shape
    return pl.pallas_call(
        _ctx_attn_kernel,
        out_shape=jax.ShapeDtypeStruct((b, lc, nh * HEAD_DIM), BF16),
        grid=(b, nh),
        in_specs=[pl.BlockSpec((1, lc, HEAD_DIM), lambda bi, h: (bi, 0, h)),
                  pl.BlockSpec((1, lc, HEAD_DIM), lambda bi, h: (bi, 0, nh + h)),
                  pl.BlockSpec((1, lc, HEAD_DIM), lambda bi, h: (bi, 0, 2 * nh + h))],
        out_specs=pl.BlockSpec((1, lc, HEAD_DIM), lambda bi, h: (bi, 0, h)),
        compiler_params=_cparams(2),
        name="ctx_attention",
    )(pa_ctx, pa_ctx, pa_ctx)


_GLA_LEVELS = (32, 16, 8, 4, 2, 1)


def _gla_dir_consts(rev):
    c = GLA_CHUNK
    ii = lax.broadcasted_iota(jnp.int32, (c, c), 0)
    jj = lax.broadcasted_iota(jnp.int32, (c, c), 1)
    tri = ((jj >= ii) if rev else (jj <= ii)).astype(BF16)
    i2 = lax.broadcasted_iota(jnp.int32, (c, 2 * c), 0)
    j2 = lax.broadcasted_iota(jnp.int32, (c, 2 * c), 1) & (c - 1)
    masks = [i2 == j2]
    for s in _GLA_LEVELS:
        same = (i2 // (2 * s)) == (j2 // (2 * s))
        qi = i2 % (2 * s)
        kj = j2 % (2 * s)
        masks.append(same & ((qi < s) & (kj >= s) if rev else (qi >= s) & (kj < s)))
    return tri, masks


def _gla_ref_rows(cum, s, rev, sub8):
    c, w = cum.shape
    blk = 2 * s
    off = s - 1 if rev else s
    pieces = []
    if blk >= 8:
        for b0 in range(0, c, blk):
            pieces.append(jnp.broadcast_to(cum[b0 + off:b0 + off + 1, :], (blk, w)))
    else:
        for g0 in range(0, c, 8):
            out = None
            for b0 in range(0, 8, blk):
                cand = jnp.broadcast_to(cum[g0 + b0 + off:g0 + b0 + off + 1, :], (8, w))
                out = cand if out is None else jnp.where(sub8 >= b0, cand, out)
            pieces.append(out)
    return jnp.concatenate(pieces, axis=0)


def _gla_chunk(rev, consts, shared, q, k, v, g, st_ref, n_pairs):
    c = GLA_CHUNK
    tri, masks = consts
    left128, left256, st_diag, sub8 = shared
    g1 = g.astype(BF16)
    r1 = g - g1.astype(F32)
    g2 = r1.astype(BF16)
    g3 = (r1 - g2.astype(F32)).astype(BF16)
    cum = _bdot(tri, g1) + _bdot(tri, g2) + _bdot(tri, g3)
    end_row = 0 if rev else c - 1
    last = cum[end_row:end_row + 1, :]
    q_in = (q * jnp.exp(cum)).astype(BF16)
    k_out = (k * jnp.exp(last - cum)).astype(BF16)
    e_last = jnp.exp(last)

    qs = [q.astype(BF16)]
    ks = [k.astype(BF16)]
    for s in _GLA_LEVELS:
        dq = cum - _gla_ref_rows(cum, s, rev, sub8)
        e = jnp.exp(jnp.minimum(dq, -dq))
        qs.append((q * e).astype(BF16))
        ks.append((k * e).astype(BF16))

    outs = []
    for p in range(n_pairs):
        ksl = slice(p * 128, (p + 1) * 128)
        vsl = slice(p * 256, (p + 1) * 256)
        attn = jnp.zeros((c, 2 * c), F32)
        for lv_i in range(len(masks)):
            kp = ks[lv_i][:, ksl]
            kbd = jnp.concatenate([jnp.where(left128, kp, 0), jnp.where(left128, 0, kp)], axis=0)
            a = _bdot_nt(qs[lv_i][:, ksl], kbd)
            attn = jnp.where(masks[lv_i], a, attn)
        vp16 = v[:, vsl].astype(BF16)
        vbd = jnp.concatenate([jnp.where(left256, vp16, 0), jnp.where(left256, 0, vp16)], axis=0)
        st = st_ref[p]
        outs.append(_bdot(attn.astype(BF16), vbd) + _bdot_nt(q_in[:, ksl], st.astype(BF16)))
        upd = lax.dot_general(vp16, k_out[:, ksl], (((0,), (0,)), ((), ())), preferred_element_type=F32)
        st_ref[p] = st * e_last[:, ksl] + jnp.where(st_diag, upd, 0.0)
    return jnp.concatenate(outs, axis=1)


def _gla_kernel(qf_ref, kf_ref, vf_ref, gf_ref, cosf_ref, sinf_ref,
                qr_ref, kr_ref, vr_ref, gr_ref, cosr_ref, sinr_ref, s0f_ref, s0r_ref,
                of_ref, or_ref, sff_ref, sfr_ref, stf_scr, str_scr, *, n_chunks, n_pairs):
    c = GLA_CHUNK
    blk_i = pl.program_id(1)

    @pl.when(blk_i == 0)
    def _():
        stf_scr[...] = s0f_ref[0]
        str_scr[...] = s0r_ref[0]

    l2 = lax.broadcasted_iota(jnp.int32, (c, 2 * c), 1)
    left128 = l2 < c
    lane32 = l2 & 31
    left256 = lax.broadcasted_iota(jnp.int32, (c, 2 * GLA_DV), 1) < GLA_DV
    sr = lax.broadcasted_iota(jnp.int32, (2 * GLA_DV, 2 * GLA_DK), 0)
    sc_ = lax.broadcasted_iota(jnp.int32, (2 * GLA_DV, 2 * GLA_DK), 1)
    st_diag = (sr // GLA_DV) == (sc_ // GLA_DK)
    sub8 = lax.broadcasted_iota(jnp.int32, (8, n_pairs * 128), 0)
    shared = (left128, left256, st_diag, sub8)
    consts_f = _gla_dir_consts(False)
    consts_r = _gla_dir_consts(True)

    def rope(x, cos, sin):
        outs = []
        for p in range(n_pairs):
            xs = x[:, p * 128:(p + 1) * 128]
            up = pltpu.roll(xs, 16, axis=1)
            dn = pltpu.roll(xs, 112, axis=1)
            sw = jnp.where(lane32 < 16, dn, up)
            outs.append(xs * cos + sw * sin)
        return jnp.concatenate(outs, axis=1)

    def one_dir(rev, rows, q_ref, k_ref, v_ref, g_ref, cos_ref, sin_ref, st_scr, o_ref):
        cos = cos_ref[rows, :]
        sin = sin_ref[rows, :]
        q = rope(q_ref[0, rows, :], cos, sin) * (GLA_DK ** -0.5)
        k = rope(k_ref[0, rows, :], cos, sin)
        consts = consts_r if rev else consts_f
        o_ref[0, rows, :] = _gla_chunk(rev, consts, shared, q, k, v_ref[0, rows, :], g_ref[0, rows, :],
                                       st_scr, n_pairs)

    def chunk(ci, carry):
        rows_f = pl.ds(pl.multiple_of(ci * c, c), c)
        rows_r = pl.ds(pl.multiple_of((n_chunks - 1 - ci) * c, c), c)
        one_dir(False, rows_f, qf_ref, kf_ref, vf_ref, gf_ref, cosf_ref, sinf_ref, stf_scr, of_ref)
        one_dir(True, rows_r, qr_ref, kr_ref, vr_ref, gr_ref, cosr_ref, sinr_ref, str_scr, or_ref)
        return carry

    lax.fori_loop(0, n_chunks, chunk, 0)

    @pl.when(blk_i == pl.num_programs(1) - 1)
    def _():
        sff_ref[0] = stf_scr[...]
        sfr_ref[0] = str_scr[...]


def _gla_call(pb, lg, cos, sin, s0f, s0r, *, tb):
    b, l, _ = pb.shape
    nqk = lg.shape[2] // 2
    n_pairs = nqk // 128
    nv = n_pairs * 2 * GLA_DV
    nb = l // tb
    kern = functools.partial(_gla_kernel, n_chunks=tb // GLA_CHUNK, n_pairs=n_pairs)
    st_shape = (n_pairs, 2 * GLA_DV, 2 * GLA_DK)

    def dir_specs(tok, gate_blk):
        return [pl.BlockSpec((1, tb, nqk), lambda bi, i: (bi, tok(i), 0)),
                pl.BlockSpec((1, tb, nqk), lambda bi, i: (bi, tok(i), 1)),
                pl.BlockSpec((1, tb, nv), lambda bi, i: (bi, tok(i), 1)),
                pl.BlockSpec((1, tb, nqk), lambda bi, i: (bi, tok(i), gate_blk)),
                pl.BlockSpec((tb, 128), lambda bi, i: (tok(i), 0)),
                pl.BlockSpec((tb, 128), lambda bi, i: (tok(i), 0))]

    fwd_tok = lambda i: i
    rev_tok = lambda i: nb - 1 - i
    st_spec = pl.BlockSpec((1,) + st_shape, lambda bi, i: (bi, 0, 0, 0))
    return pl.pallas_call(
        kern,
        out_shape=(jax.ShapeDtypeStruct((b, l, nv), F32), jax.ShapeDtypeStruct((b, l, nv), F32),
                   jax.ShapeDtypeStruct((b,) + st_shape, F32), jax.ShapeDtypeStruct((b,) + st_shape, F32)),
        grid=(b, nb),
        in_specs=dir_specs(fwd_tok, 0) + dir_specs(rev_tok, 1) + [st_spec, st_spec],
        out_specs=(pl.BlockSpec((1, tb, nv), lambda bi, i: (bi, fwd_tok(i), 0)),
                   pl.BlockSpec((1, tb, nv), lambda bi, i: (bi, rev_tok(i), 0)),
                   st_spec, st_spec),
        scratch_shapes=[pltpu.VMEM(st_shape, F32), pltpu.VMEM(st_shape, F32)],
        compiler_params=_cparams(2),
        name="gla_scan",
    )(pb, pb, pb, lg, cos, sin, pb, pb, pb, lg, cos, sin, s0f, s0r)


def _rope_tables(seq_len):
    t = jnp.arange(seq_len)
    row = (t // GRID_W).astype(F32)
    col = (t % GRID_W).astype(F32)
    nf = GLA_DK // 4
    inv = ROPE_BASE ** (-jnp.arange(nf, dtype=F32) / nf)
    ar = row[:, None] * inv[None, :]
    ac = col[:, None] * inv[None, :]
    cos = jnp.concatenate([jnp.cos(ar), jnp.cos(ar), jnp.cos(ac), jnp.cos(ac)], axis=1)
    sin = jnp.concatenate([-jnp.sin(ar), jnp.sin(ar), -jnp.sin(ac), jnp.sin(ac)], axis=1)
    return jnp.tile(cos, (1, 2)), jnp.tile(sin, (1, 2))


def _combine_kernel(oa_ref, of_ref, ob_ref, gb_ref, gg_ref, w_ref, x_ref, g1_ref, o_ref, *, n_heads):
    ob = of_ref[0] + ob_ref[0]
    parts = []
    for h in range(n_heads):
        oh = ob[:, h * GLA_DV:(h + 1) * GLA_DV]
        ms = jnp.mean(oh * oh, axis=-1, keepdims=True)
        parts.append(oh * lax.rsqrt(ms + EPS))
    obn = jnp.concatenate(parts, axis=1) * gg_ref[...]
    gb = gb_ref[0]
    yb = obn * (gb / (1.0 + jnp.exp(-gb)))
    na_w = oa_ref.shape[2]
    y = _bdot(oa_ref[0], w_ref[0:na_w, :]) + _bdot(yb.astype(BF16), w_ref[na_w:, :])
    o_ref[0] = x_ref[0] + g1_ref[0] * y


def _combine_call(oa, of, ob, pb, gla_g, w_out, x, g1, tm):
    b, l, d = x.shape
    na_w = oa.shape[2]
    gl_w = of.shape[2]
    n_heads = gl_w // GLA_DV
    gate_blk = (pb.shape[2] - gl_w) // gl_w
    kern = functools.partial(_combine_kernel, n_heads=n_heads)
    return pl.pallas_call(
        kern,
        out_shape=jax.ShapeDtypeStruct((b, l, d), F32),
        grid=(b, l // tm),
        in_specs=[pl.BlockSpec((1, tm, na_w), lambda bi, i: (bi, i, 0)),
                  pl.BlockSpec((1, tm, gl_w), lambda bi, i: (bi, i, 0)),
                  pl.BlockSpec((1, tm, gl_w), lambda bi, i: (bi, i, 0)),
                  pl.BlockSpec((1, tm, gl_w), lambda bi, i: (bi, i, gate_blk)),
                  pl.BlockSpec((1, gl_w), lambda bi, i: (0, 0)),
                  pl.BlockSpec(w_out.shape, lambda bi, i: (0, 0)),
                  pl.BlockSpec((1, tm, d), lambda bi, i: (bi, i, 0)),
                  pl.BlockSpec((1, 1, d), lambda bi, i: (bi, 0, 0))],
        out_specs=pl.BlockSpec((1, tm, d), lambda bi, i: (bi, i, 0)),
        compiler_params=_cparams(2),
        name="combine_out",
    )(oa, of, ob, pb, gla_g, w_out, x, g1)


def _pool_kernel(x_ref, xp_ref, xn_ref, g_ref, sh_ref, sc_ref, w_ref, ps_ref, g1_ref, o_ref, h_scr,
                 *, tm, seq_len):
    i = pl.program_id(1)
    n_i = pl.num_programs(1)
    g = g_ref[...]
    sh = sh_ref[0]
    sc = sc_ref[0]
    x = x_ref[0]
    hm = _norm_mod(x, g, sh, sc)
    h_scr[HALO:HALO + tm, :] = hm
    h_scr[0:HALO, :] = jnp.where(i > 0, _norm_mod(xp_ref[0], g, sh, sc), 0.0)
    h_scr[HALO + tm:, :] = jnp.where(i < n_i - 1, _norm_mod(xn_ref[0], g, sh, sc), 0.0)

    t = i * tm + lax.broadcasted_iota(jnp.int32, (tm, 1), 0)
    grp = hm.shape[1] // len(POOL_WINDOWS)
    ys = []
    for gi, w in enumerate(POOL_WINDOWS):
        cols = slice(gi * grp, (gi + 1) * grp)
        acc = None
        for off in range(-(w // 2), w // 2):
            term = h_scr[HALO + off:HALO + off + tm, cols]
            acc = term if acc is None else acc + term
        cnt = (jnp.minimum(t + w // 2, seq_len) - jnp.maximum(t - w // 2, 0)).astype(F32)
        pooled = acc / cnt - hm[:, cols]
        ys.append(_bdot(pooled.astype(BF16), w_ref[gi]))
    y = jnp.concatenate(ys, axis=1) * ps_ref[...]
    o_ref[0] = x + g1_ref[0] * y


def _halo_specs(tm, l, d):
    per = tm // HALO
    last = l // HALO - 1
    prev = pl.BlockSpec((1, HALO, d), lambda bi, i, *_: (bi, jnp.maximum(i * per - 1, 0), 0))
    nxt = pl.BlockSpec((1, HALO, d), lambda bi, i, *_: (bi, jnp.minimum((i + 1) * per, last), 0))
    return prev, nxt


def _pool_call(x, g, shift, scale, pool_w, pool_scale, g1, tm):
    b, l, d = x.shape
    prev, nxt = _halo_specs(tm, l, d)
    kern = functools.partial(_pool_kernel, tm=tm, seq_len=l)
    return pl.pallas_call(
        kern,
        out_shape=jax.ShapeDtypeStruct((b, l, d), F32),
        grid=(b, l // tm),
        in_specs=[pl.BlockSpec((1, tm, d), lambda bi, i: (bi, i, 0)), prev, nxt,
                  pl.BlockSpec((1, d), lambda bi, i: (0, 0)),
                  pl.BlockSpec((1, 1, d), lambda bi, i: (bi, 0, 0)),
                  pl.BlockSpec((1, 1, d), lambda bi, i: (bi, 0, 0)),
                  pl.BlockSpec(pool_w.shape, lambda bi, i: (0, 0, 0)),
                  pl.BlockSpec((1, d), lambda bi, i: (0, 0)),
                  pl.BlockSpec((1, 1, d), lambda bi, i: (bi, 0, 0))],
        out_specs=pl.BlockSpec((1, tm, d), lambda bi, i: (bi, i, 0)),
        scratch_shapes=[pltpu.VMEM((tm + 2 * HALO, d), F32)],
        compiler_params=_cparams(2),
        name="pool_mixer",
    )(x, x, x, g, shift, scale, pool_w, pool_scale, g1)


def _ffn_kernel(x_ref, xp_ref, xn_ref, g_ref, sh_ref, sc_ref, wv_ref, wg_ref, cw_ref, cb_ref, wd_ref,
                g2_ref, o_ref, h_scr, acc_scr, *, tm):
    i = pl.program_id(1)
    j = pl.program_id(2)
    n_i = pl.num_programs(1)

    @pl.when(j == 0)
    def _():
        g = g_ref[...]
        sh = sh_ref[0]
        sc = sc_ref[0]
        h_scr[HALO:HALO + tm, :] = _norm_mod(x_ref[0], g, sh, sc).astype(BF16)
        h_scr[0:HALO, :] = jnp.where(i > 0, _norm_mod(xp_ref[0], g, sh, sc), 0.0).astype(BF16)
        h_scr[HALO + tm:, :] = jnp.where(i < n_i - 1, _norm_mod(xn_ref[0], g, sh, sc), 0.0).astype(BF16)
        acc_scr[...] = jnp.zeros_like(acc_scr)

    rows_all = tm + 2 * HALO
    ug = _bdot(h_scr[...], wg_ref[...])
    uv = _bdot(h_scr[HALO:HALO + tm, :], wv_ref[...])
    cw = cw_ref[...]
    g_prev = pltpu.roll(ug, 1, axis=0)[HALO:HALO + tm]
    g_next = pltpu.roll(ug, rows_all - 1, axis=0)[HALO:HALO + tm]
    gate = g_prev * cw[0:1] + ug[HALO:HALO + tm] * cw[1:2] + g_next * cw[2:3] + cb_ref[...]
    act = 0.5 * gate * (1.0 + lax.erf(gate * (2.0 ** -0.5))) * uv
    acc_scr[...] += _bdot(act.astype(BF16), wd_ref[...])

    @pl.when(j == pl.num_programs(2) - 1)
    def _():
        o_ref[0] = x_ref[0] + g2_ref[0] * acc_scr[...]


def _ffn_call(x, g, shift, scale, w_up, conv_w, conv_b, w_down, g2, tm, tf):
    b, l, d = x.shape
    f = w_down.shape[0]
    nf = f // tf
    prev, nxt = _halo_specs(tm, l, d)
    kern = functools.partial(_ffn_kernel, tm=tm)
    return pl.pallas_call(
        kern,
        out_shape=jax.ShapeDtypeStruct((b, l, d), F32),
        grid=(b, l // tm, nf),
        in_specs=[pl.BlockSpec((1, tm, d), lambda bi, i, j: (bi, i, 0)), prev, nxt,
                  pl.BlockSpec((1, d), lambda bi, i, j: (0, 0)),
                  pl.BlockSpec((1, 1, d), lambda bi, i, j: (bi, 0, 0)),
                  pl.BlockSpec((1, 1, d), lambda bi, i, j: (bi, 0, 0)),
                  pl.BlockSpec((d, tf), lambda bi, i, j: (0, j)),
                  pl.BlockSpec((d, tf), lambda bi, i, j: (0, nf + j)),
                  pl.BlockSpec((3, tf), lambda bi, i, j: (0, j)),
                  pl.BlockSpec((1, tf), lambda bi, i, j: (0, j)),
                  pl.BlockSpec((tf, d), lambda bi, i, j: (j, 0)),
                  pl.BlockSpec((1, 1, d), lambda bi, i, j: (bi, 0, 0))],
        out_specs=pl.BlockSpec((1, tm, d), lambda bi, i, j: (bi, i, 0)),
        scratch_shapes=[pltpu.VMEM((tm + 2 * HALO, d), BF16), pltpu.VMEM((tm, d), F32)],
        compiler_params=_cparams(3),
        name="conv_ffn",
    )(x, x, x, g, shift, scale, w_up, w_up, conv_w, conv_b, w_down, g2)


def _final_kernel(x_ref, g_ref, o_ref):
    x = x_ref[0]
    ms = jnp.mean(x * x, axis=-1, keepdims=True)
    o_ref[0] = x * lax.rsqrt(ms + EPS) * g_ref[...]


def _final_call(x, g, tm):
    b, l, d = x.shape
    return pl.pallas_call(
        _final_kernel,
        out_shape=jax.ShapeDtypeStruct((b, l, d), F32),
        grid=(b, l // tm),
        in_specs=[pl.BlockSpec((1, tm, d), lambda bi, i: (bi, i, 0)),
                  pl.BlockSpec((1, d), lambda bi, i: (0, 0))],
        out_specs=pl.BlockSpec((1, tm, d), lambda bi, i: (bi, i, 0)),
        compiler_params=_cparams(2),
        name="final_norm",
    )(x, g)


def _tile(l, want):
    return min(l, want)


def _even_mixer(x_lat, x_ctx, mods_lat, mods_ctx, n1, w_in, w_gate2, b_gate, rpb, gla_g, w_out, need_ctx,
                rope_lat, rope_ctx):
    sh, sc, g1 = mods_lat
    shc, scc, gc1 = mods_ctx
    nh = rpb.shape[0]
    a_w = 3 * nh * HEAD_DIM
    n_gla = w_gate2.shape[2] // GLA_DK
    b_w = 2 * n_gla * GLA_DK + 2 * n_gla * GLA_DV
    w_a = w_in[:, :a_w].astype(BF16)
    w_b = w_in[:, a_w:a_w + b_w].astype(BF16)
    w_ab = jnp.pad(w_in[:, a_w + b_w:], ((0, 0), (0, 128 - 2 * GLA_RANK))).astype(BF16)
    nqk = n_gla * GLA_DK
    wg2 = jnp.zeros((128, 2 * nqk), F32)
    wg2 = wg2.at[0:GLA_RANK, 0:nqk].set(w_gate2[0]).at[GLA_RANK:2 * GLA_RANK, nqk:].set(w_gate2[1]).astype(BF16)
    bg = b_gate.reshape(1, 2 * nqk)

    l = x_lat.shape[1]
    lc = x_ctx.shape[1]
    pa, pb, lg = _proj_call(x_lat, n1, sh, sc, w_a, w_b, w_ab, wg2, bg, _tile(l, 1024), 512)
    pa_c, pb_c, lg_c = _proj_call(x_ctx, n1, shc, scc, w_a, w_b, w_ab, wg2, bg, _tile(lc, 1024), 512)

    oa = _na_call(pa, pa_c, rpb)

    bsz = x_lat.shape[0]
    s0 = jnp.zeros((bsz, nqk // 128, 2 * GLA_DV, 2 * GLA_DK), F32)
    of_c, ob_c, s_f, s_b = _gla_call(pb_c, lg_c, rope_ctx[0], rope_ctx[1], s0, s0, tb=_tile(lc, 512))
    of, ob, _, _ = _gla_call(pb, lg, rope_lat[0], rope_lat[1], s_f, s_b, tb=_tile(l, 512))

    w_o = w_out.astype(BF16)
    gg = gla_g.reshape(1, -1)
    x_lat = _combine_call(oa, of, ob, pb, gg, w_o, x_lat, g1, _tile(l, 512))
    if need_ctx:
        oa_c = _ctx_attn_call(pa_c, nh)
        x_ctx = _combine_call(oa_c, of_c, ob_c, pb_c, gg, w_o, x_ctx, gc1, _tile(lc, 512))
    return x_lat, x_ctx


def kernel(x, c, ctx, c_ctx, w_mod, b_mod, norm1_g, norm2_g, w_in, w_gate2, b_gate, rpb, gla_norm_g, w_out,
           pool_w, pool_scale, w_up, conv_w, conv_b, w_down, final_g):
    bsz, seq, d = x.shape
    lc = ctx.shape[1]
    depth = w_mod.shape[0]

    cvec = jnp.zeros((8, d), F32).at[0:bsz].set(c).at[bsz].set(c_ctx)
    mods = _mod_call(cvec, w_mod, b_mod)

    rope_lat = _rope_tables(seq)
    rope_ctx = (jnp.ones((lc, 128), F32), jnp.zeros((lc, 128), F32))

    x_lat, x_ctx = x, ctx
    for i in range(depth):
        is_even = i % 2 == 0
        need_ctx = any(j % 2 == 0 for j in range(i + 1, depth))
        m = mods[i].reshape(8, 6, d)
        lat = [m[0:bsz, k][:, None, :] for k in range(6)]
        cx = [jnp.broadcast_to(m[bsz, k][None, None, :], (bsz, 1, d)) for k in range(6)]
        n1 = norm1_g[i].reshape(1, d)
        n2 = norm2_g[i].reshape(1, d)
        wu = w_up[i].astype(BF16)
        wd = w_down[i].astype(BF16)
        cb = conv_b[i].reshape(1, -1)
        if is_even:
            e = i // 2
            x_lat, x_ctx = _even_mixer(x_lat, x_ctx, lat[0:3], cx[0:3], n1, w_in[e], w_gate2[e], b_gate[e], rpb[e],
                                       gla_norm_g[e], w_out[e], need_ctx, rope_lat, rope_ctx)
        else:
            o = i // 2
            pw = pool_w[o].astype(BF16)
            ps = pool_scale[o].reshape(1, d)
            x_lat = _pool_call(x_lat, n1, lat[0], lat[1], pw, ps, lat[2], _tile(seq, 512))
            if need_ctx:
                x_ctx = _pool_call(x_ctx, n1, cx[0], cx[1], pw, ps, cx[2], _tile(lc, 512))
        x_lat = _ffn_call(x_lat, n2, lat[3], lat[4], wu, conv_w[i], cb, wd, lat[5], _tile(seq, 512), 512)
        if need_ctx:
            x_ctx = _ffn_call(x_ctx, n2, cx[3], cx[4], wu, conv_w[i], cb, wd, cx[5], _tile(lc, 512), 512)
    return _final_call(x_lat, final_g.reshape(1, d), _tile(seq, 512))
```

```python
import functools

import numpy as np
import jax
import jax.numpy as jnp
from jax import lax
from jax.experimental import pallas as pl
from jax.experimental.pallas import tpu as pltpu

F32 = jnp.float32
BF16 = jnp.bfloat16

GRID_W = 64
HEAD_DIM = 128
NA_ROWS = 8
NA_COLS = 16
NA_GROUP = 4
GLA_DK = 64
GLA_DV = 128
GLA_RANK = 16
GLA_TAU = 16.0
GLA_CHUNK = 64
POOL_WINDOWS = (2, 4, 8, 16)
ROPE_BASE = 10000.0
EPS = 1e-6
HALO = 16
VMEM_LIMIT = 56 * 1024 * 1024


def _cparams(n_axes):
    return pltpu.CompilerParams(dimension_semantics=("arbitrary",) * n_axes,
                                vmem_limit_bytes=VMEM_LIMIT)


def _bdot(a, b):
    return jnp.dot(a, b, preferred_element_type=F32)


def _bdot_nt(a, b):
    return lax.dot_general(a, b, (((1,), (1,)), ((), ())), preferred_element_type=F32)


def _norm_mod(x, g, shift, scale):
    ms = jnp.mean(x * x, axis=-1, keepdims=True)
    return x * lax.rsqrt(ms + EPS) * (g * (1.0 + scale)) + shift


def _mod_kernel(c_ref, w_ref, b_ref, o_ref):
    c = c_ref[...]
    s = c / (1.0 + jnp.exp(-c))
    o_ref[0] = _bdot(s.astype(BF16), w_ref[0].astype(BF16)) + b_ref[0]


def _mod_call(cvec, w_mod, b_mod):
    depth, d, n = w_mod.shape
    tn = 1024
    return pl.pallas_call(
        _mod_kernel,
        out_shape=jax.ShapeDtypeStruct((depth, 8, n), F32),
        grid=(depth, n // tn),
        in_specs=[pl.BlockSpec((8, d), lambda l, j: (0, 0)),
                  pl.BlockSpec((1, d, tn), lambda l, j: (l, 0, j)),
                  pl.BlockSpec((1, 1, tn), lambda l, j: (l, 0, j))],
        out_specs=pl.BlockSpec((1, 8, tn), lambda l, j: (l, 0, j)),
        compiler_params=_cparams(2),
        name="mod_matvec",
    )(cvec, w_mod, b_mod.reshape(depth, 1, n))


def _proj_kernel(x_ref, g_ref, sh_ref, sc_ref, wa_ref, wb_ref, wab_ref, wg2_ref, bg_ref,
                 oa_ref, ob_ref, lg_ref, h_scr, *, na_tiles):
    j = pl.program_id(2)

    @pl.when(j == 0)
    def _():
        h = _norm_mod(x_ref[0], g_ref[...], sh_ref[0], sc_ref[0]).astype(BF16)
        h_scr[...] = h
        ab = _bdot(h, wab_ref[...])
        z = _bdot(ab.astype(BF16), wg2_ref[...]) + bg_ref[...]
        log_sig = jnp.minimum(z, 0.0) - jnp.log1p(jnp.exp(-jnp.abs(z)))
        lg_ref[0] = log_sig * (1.0 / GLA_TAU)

    @pl.when(j < na_tiles)
    def _():
        oa_ref[0] = _bdot(h_scr[...], wa_ref[0]).astype(oa_ref.dtype)

    @pl.when(j >= na_tiles)
    def _():
        ob_ref[0] = _bdot(h_scr[...], wb_ref[0])


def _proj_call(x, g, shift, scale, w_a, w_b, wab, wg2, bg, tm):
    b, l, d = x.shape
    ta, _, tn = w_a.shape
    tb_ = w_b.shape[0]
    na, nb_ = ta * tn, tb_ * tn
    ng = wg2.shape[1]
    kern = functools.partial(_proj_kernel, na_tiles=ta)
    return pl.pallas_call(
        kern,
        out_shape=(jax.ShapeDtypeStruct((b, l, na), BF16), jax.ShapeDtypeStruct((b, l, nb_), F32),
                   jax.ShapeDtypeStruct((b, l, ng), F32)),
        grid=(b, l // tm, ta + tb_),
        in_specs=[pl.BlockSpec((1, tm, d), lambda bi, i, j: (bi, i, 0)),
                  pl.BlockSpec((1, d), lambda bi, i, j: (0, 0)),
                  pl.BlockSpec((1, 1, d), lambda bi, i, j: (bi, 0, 0)),
                  pl.BlockSpec((1, 1, d), lambda bi, i, j: (bi, 0, 0)),
                  pl.BlockSpec((1, d, tn), lambda bi, i, j: (jnp.minimum(j, ta - 1), 0, 0)),
                  pl.BlockSpec((1, d, tn), lambda bi, i, j: (jnp.maximum(j - ta, 0), 0, 0)),
                  pl.BlockSpec(wab.shape, lambda bi, i, j: (0, 0)),
                  pl.BlockSpec(wg2.shape, lambda bi, i, j: (0, 0)),
                  pl.BlockSpec((1, ng), lambda bi, i, j: (0, 0))],
        out_specs=(pl.BlockSpec((1, tm, tn), lambda bi, i, j: (bi, i, jnp.minimum(j, ta - 1))),
                   pl.BlockSpec((1, tm, tn), lambda bi, i, j: (bi, i, jnp.maximum(j - ta, 0))),
                   pl.BlockSpec((1, tm, ng), lambda bi, i, j: (bi, i, 0))),
        scratch_shapes=[pltpu.VMEM((tm, d), BF16)],
        compiler_params=_cparams(3),
        name="proj_in",
    )(x, g, shift, scale, w_a, w_b, wab, wg2, bg)


def _na_kernel(rpb_ref, q_ref, k_ref, v_ref, kc_ref, vc_ref, o_ref, tab_ref, *, rows, n_dr, n_dc):
    head = pl.program_id(0)
    scale = HEAD_DIM ** -0.5
    wr = NA_ROWS

    qc = lax.broadcasted_iota(jnp.int32, (GRID_W, 2 * GRID_W), 0)
    lane = lax.broadcasted_iota(jnp.int32, (GRID_W, 2 * GRID_W), 1)
    kc = lane & (GRID_W - 1)
    dc = jnp.clip(kc - qc + (NA_COLS - 1), 0, n_dc - 1)
    cstart = jnp.clip(qc - NA_COLS // 2, 0, GRID_W - NA_COLS)
    in_win = (kc >= cstart) & (kc < cstart + NA_COLS)
    left = lane < GRID_W
    base_off = head * (n_dr * n_dc)

    @pl.when(pl.program_id(1) == 0)
    def _():
        for d in range(-1, n_dr):
            acc = jnp.zeros((GRID_W, 2 * GRID_W), F32)
            for c in range(n_dc):
                if d < 0:
                    val = rpb_ref[base_off + (d + 1) * n_dc + c]
                elif d + 1 >= n_dr:
                    val = rpb_ref[base_off + d * n_dc + c]
                else:
                    val = jnp.where(left, rpb_ref[base_off + d * n_dc + c], rpb_ref[base_off + (d + 1) * n_dc + c])
                acc = jnp.where(dc == c, val, acc)
            keep = in_win & ~left if d < 0 else (in_win & left if d + 1 >= n_dr else in_win)
            tab_ref[d + 1] = jnp.where(keep, acc, -jnp.inf)

    kctx = kc_ref[0]
    vctx = vc_ref[0]
    grp = NA_GROUP
    union = grp + wr
    neg_inf = jnp.full((GRID_W, 2 * GRID_W), -jnp.inf, F32)

    def row_group(gi, carry):
        r0 = gi * grp
        ustart = jnp.clip(r0 - wr // 2, 0, rows - union)
        q_rows = pl.ds(pl.multiple_of(r0 * GRID_W, grp * GRID_W), grp * GRID_W)
        kv_rows = pl.ds(pl.multiple_of(ustart * GRID_W, GRID_W), union * GRID_W)
        q = q_ref[0, q_rows, :]
        kb = k_ref[0, kv_rows, :]
        vb = v_ref[0, kv_rows, :]
        biases = []
        for g in range(grp):
            rq = r0 + g
            rstart = jnp.clip(rq - wr // 2, 0, rows - wr)
            tiles = []
            for j in range(union // 2):
                ku = ustart + 2 * j
                d = ku - rq + (NA_ROWS - 1)
                ok_l = ((ku >= rstart) & (ku < rstart + wr)).astype(jnp.int32)
                ok_r = ((ku + 1 >= rstart) & (ku + 1 < rstart + wr)).astype(jnp.int32)
                tile = tab_ref[jnp.clip(d, -1, n_dr - 1) + 1]
                tiles.append(jnp.where(jnp.where(left, ok_l, ok_r) > 0, tile, neg_inf))
            biases.append(jnp.concatenate(tiles, axis=1))
        s_loc = _bdot_nt(q, kb) * scale + jnp.concatenate(biases, axis=0)
        s_ctx = _bdot_nt(q, kctx) * scale
        m = jnp.maximum(jnp.max(s_loc, axis=-1, keepdims=True), jnp.max(s_ctx, axis=-1, keepdims=True))
        p_loc = jnp.exp(s_loc - m)
        p_ctx = jnp.exp(s_ctx - m)
        denom = jnp.sum(p_loc, axis=-1, keepdims=True) + jnp.sum(p_ctx, axis=-1, keepdims=True)
        o = _bdot(p_loc.astype(BF16), vb) + _bdot(p_ctx.astype(BF16), vctx)
        o_ref[0, q_rows, :] = (o / denom).astype(o_ref.dtype)
        return carry

    lax.fori_loop(0, rows // grp, row_group, 0, unroll=2)


def _na_call(pa, pa_ctx, rpb):
    b, l, _ = pa.shape
    lc = pa_ctx.shape[1]
    nh, n_dr, n_dc = rpb.shape
    rows = l // GRID_W
    kern = functools.partial(_na_kernel, rows=rows, n_dr=n_dr, n_dc=n_dc)
    return pl.pallas_call(
        kern,
        out_shape=jax.ShapeDtypeStruct((b, l, nh * HEAD_DIM), BF16),
        grid=(nh, b),
        in_specs=[pl.BlockSpec(memory_space=pltpu.SMEM),
                  pl.BlockSpec((1, l, HEAD_DIM), lambda h, bi: (bi, 0, h)),
                  pl.BlockSpec((1, l, HEAD_DIM), lambda h, bi: (bi, 0, nh + h)),
                  pl.BlockSpec((1, l, HEAD_DIM), lambda h, bi: (bi, 0, 2 * nh + h)),
                  pl.BlockSpec((1, lc, HEAD_DIM), lambda h, bi: (bi, 0, nh + h)),
                  pl.BlockSpec((1, lc, HEAD_DIM), lambda h, bi: (bi, 0, 2 * nh + h))],
        out_specs=pl.BlockSpec((1, l, HEAD_DIM), lambda h, bi: (bi, 0, h)),
        scratch_shapes=[pltpu.VMEM((n_dr + 1, GRID_W, 2 * GRID_W), F32)],
        compiler_params=_cparams(2),
        name="na_attention",
    )(rpb.reshape(-1), pa, pa, pa, pa_ctx, pa_ctx)


def _ctx_attn_kernel(q_ref, k_ref, v_ref, o_ref):
    q = q_ref[0]
    s = _bdot_nt(q, k_ref[0]) * (HEAD_DIM ** -0.5)
    m = jnp.max(s, axis=-1, keepdims=True)
    p = jnp.exp(s - m)
    denom = jnp.sum(p, axis=-1, keepdims=True)
    o_ref[0] = (_bdot(p.astype(BF16), v_ref[0]) / denom).astype(o_ref.dtype)


def _ctx_attn_call(pa_ctx, nh):
    b, lc, _ = pa_ctx.shape
    return pl.pallas_call(
        _ctx_attn_kernel,
        out_shape=jax.ShapeDtypeStruct((b, lc, nh * HEAD_DIM), BF16),
        grid=(b, nh),
        in_specs=[pl.BlockSpec((1, lc, HEAD_DIM), lambda bi, h: (bi, 0, h)),
                  pl.BlockSpec((1, lc, HEAD_DIM), lambda bi, h: (bi, 0, nh + h)),
                  pl.BlockSpec((1, lc, HEAD_DIM), lambda bi, h: (bi, 0, 2 * nh + h))],
        out_specs=pl.BlockSpec((1, lc, HEAD_DIM), lambda bi, h: (bi, 0, h)),
        compiler_params=_cparams(2),
        name="ctx_attention",
    )(pa_ctx, pa_ctx, pa_ctx)


_GLA_LEVELS = (32, 16, 8, 4, 2, 1)


def _gla_dir_consts(rev):
    c = GLA_CHUNK
    ii = lax.broadcasted_iota(jnp.int32, (c, c), 0)
    jj = lax.broadcasted_iota(jnp.int32, (c, c), 1)
    tri = ((jj >= ii) if rev else (jj <= ii)).astype(BF16)
    i2 = lax.broadcasted_iota(jnp.int32, (c, 2 * c), 0)
    j2 = lax.broadcasted_iota(jnp.int32, (c, 2 * c), 1) & (c - 1)
    masks = [i2 == j2]
    for s in _GLA_LEVELS:
        same = (i2 // (2 * s)) == (j2 // (2 * s))
        qi = i2 % (2 * s)
        kj = j2 % (2 * s)
        masks.append(same & ((qi < s) & (kj >= s) if rev else (qi >= s) & (kj < s)))
    return tri, masks


def _gla_ref_rows(cum, s, rev, sub8):
    c, w = cum.shape
    blk = 2 * s
    off = s - 1 if rev else s
    pieces = []
    if blk >= 8:
        for b0 in range(0, c, blk):
            pieces.append(jnp.broadcast_to(cum[b0 + off:b0 + off + 1, :], (blk, w)))
    else:
        for g0 in range(0, c, 8):
            out = None
            for b0 in range(0, 8, blk):
                cand = jnp.broadcast_to(cum[g0 + b0 + off:g0 + b0 + off + 1, :], (8, w))
                out = cand if out is None else jnp.where(sub8 >= b0, cand, out)
            pieces.append(out)
    return jnp.concatenate(pieces, axis=0)


def _gla_prep(rev, tri, sub8, q, k, g):
    c = GLA_CHUNK
    g1 = g.astype(BF16)
    r1 = g - g1.astype(F32)
    g2 = r1.astype(BF16)
    g3 = (r1 - g2.astype(F32)).astype(BF16)
    cum = _bdot(tri, g1) + _bdot(tri, g2) + _bdot(tri, g3)
    end_row = 0 if rev else c - 1
    last = cum[end_row:end_row + 1, :]
    q_in = (q * jnp.exp(cum)).astype(BF16)
    k_out = (k * jnp.exp(last - cum)).astype(BF16)
    e_last = jnp.exp(last)

    qs = [q.astype(BF16)]
    ks = [k.astype(BF16)]
    for s in _GLA_LEVELS:
        dq = cum - _gla_ref_rows(cum, s, rev, sub8)
        e = jnp.exp(jnp.minimum(dq, -dq))
        qs.append((q * e).astype(BF16))
        ks.append((k * e).astype(BF16))
    return qs, ks, q_in, k_out, e_last


def _gla_scores(masks, left128, qs, ks, n_pairs):
    c = GLA_CHUNK
    attns = [jnp.zeros((c, 2 * c), F32) for _ in range(n_pairs)]
    for lv_i in range(len(masks)):
        for p in range(n_pairs):
            ksl = slice(p * 128, (p + 1) * 128)
            kp = ks[lv_i][:, ksl]
            kbd = jnp.concatenate([jnp.where(left128, kp, 0), jnp.where(left128, 0, kp)], axis=0)
            attns[p] = jnp.where(masks[lv_i], _bdot_nt(qs[lv_i][:, ksl], kbd), attns[p])
    return [a.astype(BF16) for a in attns]


def _gla_finish(left256, st_diag, attns, q_in, k_out, e_last, v, st_ref, n_pairs):
    outs = []
    for p in range(n_pairs):
        ksl = slice(p * 128, (p + 1) * 128)
        vp16 = v[:, p * 256:(p + 1) * 256].astype(BF16)
        vbd = jnp.concatenate([jnp.where(left256, vp16, 0), jnp.where(left256, 0, vp16)], axis=0)
        st = st_ref[p]
        outs.append(_bdot(attns[p], vbd) + _bdot_nt(q_in[:, ksl], st.astype(BF16)))
        upd = lax.dot_general(vp16, k_out[:, ksl], (((0,), (0,)), ((), ())), preferred_element_type=F32)
        st_ref[p] = st * e_last[:, ksl] + jnp.where(st_diag, upd, 0.0)
    return jnp.concatenate(outs, axis=1)


def _gla_kernel(qf_ref, kf_ref, vf_ref, gf_ref, cosf_ref, sinf_ref,
                qr_ref, kr_ref, vr_ref, gr_ref, cosr_ref, sinr_ref, s0f_ref, s0r_ref,
                of_ref, or_ref, sff_ref, sfr_ref, stf_scr, str_scr, *, n_chunks, n_pairs):
    c = GLA_CHUNK
    blk_i = pl.program_id(1)

    @pl.when(blk_i == 0)
    def _():
        stf_scr[...] = s0f_ref[0]
        str_scr[...] = s0r_ref[0]

    l2 = lax.broadcasted_iota(jnp.int32, (c, 2 * c), 1)
    left128 = l2 < c
    lane32 = l2 & 31
    left256 = lax.broadcasted_iota(jnp.int32, (c, 2 * GLA_DV), 1) < GLA_DV
    sr = lax.broadcasted_iota(jnp.int32, (2 * GLA_DV, 2 * GLA_DK), 0)
    sc_ = lax.broadcasted_iota(jnp.int32, (2 * GLA_DV, 2 * GLA_DK), 1)
    st_diag = (sr // GLA_DV) == (sc_ // GLA_DK)
    sub8 = lax.broadcasted_iota(jnp.int32, (8, n_pairs * 128), 0)
    tri_f, masks_f = _gla_dir_consts(False)
    tri_r, masks_r = _gla_dir_consts(True)

    def rope(x, cos, sin):
        outs = []
        for p in range(n_pairs):
            xs = x[:, p * 128:(p + 1) * 128]
            up = pltpu.roll(xs, 16, axis=1)
            dn = pltpu.roll(xs, 112, axis=1)
            sw = jnp.where(lane32 < 16, dn, up)
            outs.append(xs * cos + sw * sin)
        return jnp.concatenate(outs, axis=1)

    def prep(rev, tri, rows, q_ref, k_ref, g_ref, cos_ref, sin_ref):
        cos = cos_ref[rows, :]
        sin = sin_ref[rows, :]
        q = rope(q_ref[0, rows, :], cos, sin) * (GLA_DK ** -0.5)
        k = rope(k_ref[0, rows, :], cos, sin)
        return _gla_prep(rev, tri, sub8, q, k, g_ref[0, rows, :])

    def chunk(ci, carry):
        rows_f = pl.ds(pl.multiple_of(ci * c, c), c)
        rows_r = pl.ds(pl.multiple_of((n_chunks - 1 - ci) * c, c), c)
        qs_f, ks_f, qin_f, kout_f, el_f = prep(False, tri_f, rows_f, qf_ref, kf_ref, gf_ref, cosf_ref, sinf_ref)
        qs_r, ks_r, qin_r, kout_r, el_r = prep(True, tri_r, rows_r, qr_ref, kr_ref, gr_ref, cosr_ref, sinr_ref)
        at_f = _gla_scores(masks_f, left128, qs_f, ks_f, n_pairs)
        at_r = _gla_scores(masks_r, left128, qs_r, ks_r, n_pairs)
        of_ref[0, rows_f, :] = _gla_finish(left256, st_diag, at_f, qin_f, kout_f, el_f, vf_ref[0, rows_f, :],
                                           stf_scr, n_pairs)
        or_ref[0, rows_r, :] = _gla_finish(left256, st_diag, at_r, qin_r, kout_r, el_r, vr_ref[0, rows_r, :],
                                           str_scr, n_pairs)
        return carry

    lax.fori_loop(0, n_chunks, chunk, 0)

    @pl.when(blk_i == pl.num_programs(1) - 1)
    def _():
        sff_ref[0] = stf_scr[...]
        sfr_ref[0] = str_scr[...]


def _gla_call(pb, lg, cos, sin, s0f, s0r, *, tb):
    b, l, _ = pb.shape
    nqk = lg.shape[2] // 2
    n_pairs = nqk // 128
    nv = n_pairs * 2 * GLA_DV
    nb = l // tb
    kern = functools.partial(_gla_kernel, n_chunks=tb // GLA_CHUNK, n_pairs=n_pairs)
    st_shape = (n_pairs, 2 * GLA_DV, 2 * GLA_DK)

    def dir_specs(tok, gate_blk):
        return [pl.BlockSpec((1, tb, nqk), lambda bi, i: (bi, tok(i), 0)),
                pl.BlockSpec((1, tb, nqk), lambda bi, i: (bi, tok(i), 1)),
                pl.BlockSpec((1, tb, nv), lambda bi, i: (bi, tok(i), 1)),
                pl.BlockSpec((1, tb, nqk), lambda bi, i: (bi, tok(i), gate_blk)),
                pl.BlockSpec((tb, 128), lambda bi, i: (tok(i), 0)),
                pl.BlockSpec((tb, 128), lambda bi, i: (tok(i), 0))]

    fwd_tok = lambda i: i
    rev_tok = lambda i: nb - 1 - i
    st_spec = pl.BlockSpec((1,) + st_shape, lambda bi, i: (bi, 0, 0, 0))
    return pl.pallas_call(
        kern,
        out_shape=(jax.ShapeDtypeStruct((b, l, nv), F32), jax.ShapeDtypeStruct((b, l, nv), F32),
                   jax.ShapeDtypeStruct((b,) + st_shape, F32), jax.ShapeDtypeStruct((b,) + st_shape, F32)),
        grid=(b, nb),
        in_specs=dir_specs(fwd_tok, 0) + dir_specs(rev_tok, 1) + [st_spec, st_spec],
        out_specs=(pl.BlockSpec((1, tb, nv), lambda bi, i: (bi, fwd_tok(i), 0)),
                   pl.BlockSpec((1, tb, nv), lambda bi, i: (bi, rev_tok(i), 0)),
                   st_spec, st_spec),
        scratch_shapes=[pltpu.VMEM(st_shape, F32), pltpu.VMEM(st_shape, F32)],
        compiler_params=_cparams(2),
        name="gla_scan",
    )(pb, pb, pb, lg, cos, sin, pb, pb, pb, lg, cos, sin, s0f, s0r)


def _rope_tables(seq_len):
    t = jnp.arange(seq_len)
    row = (t // GRID_W).astype(F32)
    col = (t % GRID_W).astype(F32)
    nf = GLA_DK // 4
    inv = ROPE_BASE ** (-jnp.arange(nf, dtype=F32) / nf)
    ar = row[:, None] * inv[None, :]
    ac = col[:, None] * inv[None, :]
    cos = jnp.concatenate([jnp.cos(ar), jnp.cos(ar), jnp.cos(ac), jnp.cos(ac)], axis=1)
    sin = jnp.concatenate([-jnp.sin(ar), jnp.sin(ar), -jnp.sin(ac), jnp.sin(ac)], axis=1)
    return jnp.tile(cos, (1, 2)), jnp.tile(sin, (1, 2))


def _combine_kernel(oa_ref, of_ref, ob_ref, gb_ref, gg_ref, w_ref, x_ref, g1_ref, o_ref, *, n_heads):
    ob = of_ref[0] + ob_ref[0]
    parts = []
    for h in range(n_heads):
        oh = ob[:, h * GLA_DV:(h + 1) * GLA_DV]
        ms = jnp.mean(oh * oh, axis=-1, keepdims=True)
        parts.append(oh * lax.rsqrt(ms + EPS))
    obn = jnp.concatenate(parts, axis=1) * gg_ref[...]
    gb = gb_ref[0]
    yb = obn * (gb / (1.0 + jnp.exp(-gb)))
    na_w = oa_ref.shape[2]
    y = _bdot(oa_ref[0], w_ref[0:na_w, :]) + _bdot(yb.astype(BF16), w_ref[na_w:, :])
    o_ref[0] = x_ref[0] + g1_ref[0] * y


def _combine_call(oa, of, ob, pb, gla_g, w_out, x, g1, tm):
    b, l, d = x.shape
    na_w = oa.shape[2]
    gl_w = of.shape[2]
    n_heads = gl_w // GLA_DV
    gate_blk = (pb.shape[2] - gl_w) // gl_w
    kern = functools.partial(_combine_kernel, n_heads=n_heads)
    return pl.pallas_call(
        kern,
        out_shape=jax.ShapeDtypeStruct((b, l, d), F32),
        grid=(b, l // tm),
        in_specs=[pl.BlockSpec((1, tm, na_w), lambda bi, i: (bi, i, 0)),
                  pl.BlockSpec((1, tm, gl_w), lambda bi, i: (bi, i, 0)),
                  pl.BlockSpec((1, tm, gl_w), lambda bi, i: (bi, i, 0)),
                  pl.BlockSpec((1, tm, gl_w), lambda bi, i: (bi, i, gate_blk)),
                  pl.BlockSpec((1, gl_w), lambda bi, i: (0, 0)),
                  pl.BlockSpec(w_out.shape, lambda bi, i: (0, 0)),
                  pl.BlockSpec((1, tm, d), lambda bi, i: (bi, i, 0)),
                  pl.BlockSpec((1, 1, d), lambda bi, i: (bi, 0, 0))],
        out_specs=pl.BlockSpec((1, tm, d), lambda bi, i: (bi, i, 0)),
        compiler_params=_cparams(2),
        name="combine_out",
    )(oa, of, ob, pb, gla_g, w_out, x, g1)


def _pool_kernel(x_ref, xp_ref, xn_ref, g_ref, sh_ref, sc_ref, w_ref, ps_ref, g1_ref, o_ref, h_scr,
                 *, tm, seq_len):
    i = pl.program_id(1)
    n_i = pl.num_programs(1)
    g = g_ref[...]
    sh = sh_ref[0]
    sc = sc_ref[0]
    x = x_ref[0]
    hm = _norm_mod(x, g, sh, sc)
    h_scr[HALO:HALO + tm, :] = hm
    h_scr[0:HALO, :] = jnp.where(i > 0, _norm_mod(xp_ref[0], g, sh, sc), 0.0)
    h_scr[HALO + tm:, :] = jnp.where(i < n_i - 1, _norm_mod(xn_ref[0], g, sh, sc), 0.0)

    t = i * tm + lax.broadcasted_iota(jnp.int32, (tm, 1), 0)
    grp = hm.shape[1] // len(POOL_WINDOWS)
    ys = []
    for gi, w in enumerate(POOL_WINDOWS):
        cols = slice(gi * grp, (gi + 1) * grp)
        acc = None
        for off in range(-(w // 2), w // 2):
            term = h_scr[HALO + off:HALO + off + tm, cols]
            acc = term if acc is None else acc + term
        cnt = (jnp.minimum(t + w // 2, seq_len) - jnp.maximum(t - w // 2, 0)).astype(F32)
        pooled = acc / cnt - hm[:, cols]
        ys.append(_bdot(pooled.astype(BF16), w_ref[gi]))
    y = jnp.concatenate(ys, axis=1) * ps_ref[...]
    o_ref[0] = x + g1_ref[0] * y


def _halo_specs(tm, l, d):
    per = tm // HALO
    last = l // HALO - 1
    prev = pl.BlockSpec((1, HALO, d), lambda bi, i, *_: (bi, jnp.maximum(i * per - 1, 0), 0))
    nxt = pl.BlockSpec((1, HALO, d), lambda bi, i, *_: (bi, jnp.minimum((i + 1) * per, last), 0))
    return prev, nxt


def _pool_call(x, g, shift, scale, pool_w, pool_scale, g1, tm):
    b, l, d = x.shape
    prev, nxt = _halo_specs(tm, l, d)
    kern = functools.partial(_pool_kernel, tm=tm, seq_len=l)
    return pl.pallas_call(
        kern,
        out_shape=jax.ShapeDtypeStruct((b, l, d), F32),
        grid=(b, l // tm),
        in_specs=[pl.BlockSpec((1, tm, d), lambda bi, i: (bi, i, 0)), prev, nxt,
                  pl.BlockSpec((1, d), lambda bi, i: (0, 0)),
                  pl.BlockSpec((1, 1, d), lambda bi, i: (bi, 0, 0)),
                  pl.BlockSpec((1, 1, d), lambda bi, i: (bi, 0, 0)),
                  pl.BlockSpec(pool_w.shape, lambda bi, i: (0, 0, 0)),
                  pl.BlockSpec((1, d), lambda bi, i: (0, 0)),
                  pl.BlockSpec((1, 1, d), lambda bi, i: (bi, 0, 0))],
        out_specs=pl.BlockSpec((1, tm, d), lambda bi, i: (bi, i, 0)),
        scratch_shapes=[pltpu.VMEM((tm + 2 * HALO, d), F32)],
        compiler_params=_cparams(2),
        name="pool_mixer",
    )(x, x, x, g, shift, scale, pool_w, pool_scale, g1)


def _ffn_kernel(x_ref, xp_ref, xn_ref, g_ref, sh_ref, sc_ref, wv_ref, wg_ref, cw_ref, cb_ref, wd_ref,
                g2_ref, fg_ref, o_ref, h_scr, acc_scr, *, tm, final_norm):
    i = pl.program_id(1)
    j = pl.program_id(2)
    n_i = pl.num_programs(1)

    @pl.when(j == 0)
    def _():
        g = g_ref[...]
        sh = sh_ref[0]
        sc = sc_ref[0]
        h_scr[HALO:HALO + tm, :] = _norm_mod(x_ref[0], g, sh, sc).astype(BF16)
        h_scr[0:HALO, :] = jnp.where(i > 0, _norm_mod(xp_ref[0], g, sh, sc), 0.0).astype(BF16)
        h_scr[HALO + tm:, :] = jnp.where(i < n_i - 1, _norm_mod(xn_ref[0], g, sh, sc), 0.0).astype(BF16)
        acc_scr[...] = jnp.zeros_like(acc_scr)

    rows_all = tm + 2 * HALO
    ug = _bdot(h_scr[...], wg_ref[0])
    uv = _bdot(h_scr[HALO:HALO + tm, :], wv_ref[0])
    cw = cw_ref[...]
    g_prev = pltpu.roll(ug, 1, axis=0)[HALO:HALO + tm]
    g_next = pltpu.roll(ug, rows_all - 1, axis=0)[HALO:HALO + tm]
    gate = g_prev * cw[0:1] + ug[HALO:HALO + tm] * cw[1:2] + g_next * cw[2:3] + cb_ref[...]
    act = 0.5 * gate * (1.0 + lax.erf(gate * (2.0 ** -0.5))) * uv
    acc_scr[...] += _bdot(act.astype(BF16), wd_ref[...])

    @pl.when(j == pl.num_programs(2) - 1)
    def _():
        y = x_ref[0] + g2_ref[0] * acc_scr[...]
        if final_norm:
            ms = jnp.mean(y * y, axis=-1, keepdims=True)
            y = y * lax.rsqrt(ms + EPS) * fg_ref[...]
        o_ref[0] = y


def _ffn_call(x, g, shift, scale, w_up, conv_w, conv_b, w_down, g2, final_g, tm, final_norm):
    b, l, d = x.shape
    nf = w_up.shape[0] // 2
    tf = w_up.shape[2]
    prev, nxt = _halo_specs(tm, l, d)
    kern = functools.partial(_ffn_kernel, tm=tm, final_norm=final_norm)
    return pl.pallas_call(
        kern,
        out_shape=jax.ShapeDtypeStruct((b, l, d), F32),
        grid=(b, l // tm, nf),
        in_specs=[pl.BlockSpec((1, tm, d), lambda bi, i, j: (bi, i, 0)), prev, nxt,
                  pl.BlockSpec((1, d), lambda bi, i, j: (0, 0)),
                  pl.BlockSpec((1, 1, d), lambda bi, i, j: (bi, 0, 0)),
                  pl.BlockSpec((1, 1, d), lambda bi, i, j: (bi, 0, 0)),
                  pl.BlockSpec((1, d, tf), lambda bi, i, j: (j, 0, 0)),
                  pl.BlockSpec((1, d, tf), lambda bi, i, j: (nf + j, 0, 0)),
                  pl.BlockSpec((3, tf), lambda bi, i, j: (0, j)),
                  pl.BlockSpec((1, tf), lambda bi, i, j: (0, j)),
                  pl.BlockSpec((tf, d), lambda bi, i, j: (j, 0)),
                  pl.BlockSpec((1, 1, d), lambda bi, i, j: (bi, 0, 0)),
                  pl.BlockSpec((1, d), lambda bi, i, j: (0, 0))],
        out_specs=pl.BlockSpec((1, tm, d), lambda bi, i, j: (bi, i, 0)),
        scratch_shapes=[pltpu.VMEM((tm + 2 * HALO, d), BF16), pltpu.VMEM((tm, d), F32)],
        compiler_params=_cparams(3),
        name="conv_ffn",
    )(x, x, x, g, shift, scale, w_up, w_up, conv_w, conv_b, w_down, g2, final_g)


def _tile(l, want):
    return min(l, want)


def _col_tiles(w, tn):
    k, n = w.shape
    return w.reshape(k, n // tn, tn).transpose(1, 0, 2)


def _even_mixer(x_lat, x_ctx, mods_lat, mods_ctx, n1, w_in, w_gate2, b_gate, rpb, gla_g, w_out, need_ctx,
                rope_lat, rope_ctx):
    sh, sc, g1 = mods_lat
    shc, scc, gc1 = mods_ctx
    nh = rpb.shape[0]
    a_w = 3 * nh * HEAD_DIM
    n_gla = w_gate2.shape[2] // GLA_DK
    b_w = 2 * n_gla * GLA_DK + 2 * n_gla * GLA_DV
    w_a = _col_tiles(w_in[:, :a_w].astype(BF16), 512)
    w_b = _col_tiles(w_in[:, a_w:a_w + b_w].astype(BF16), 512)
    w_ab = jnp.pad(w_in[:, a_w + b_w:], ((0, 0), (0, 128 - 2 * GLA_RANK))).astype(BF16)
    nqk = n_gla * GLA_DK
    wg2 = jnp.zeros((128, 2 * nqk), F32)
    wg2 = wg2.at[0:GLA_RANK, 0:nqk].set(w_gate2[0]).at[GLA_RANK:2 * GLA_RANK, nqk:].set(w_gate2[1]).astype(BF16)
    bg = b_gate.reshape(1, 2 * nqk)

    l = x_lat.shape[1]
    lc = x_ctx.shape[1]
    pa, pb, lg = _proj_call(x_lat, n1, sh, sc, w_a, w_b, w_ab, wg2, bg, _tile(l, 1024))
    pa_c, pb_c, lg_c = _proj_call(x_ctx, n1, shc, scc, w_a, w_b, w_ab, wg2, bg, _tile(lc, 1024))

    oa = _na_call(pa, pa_c, rpb)

    bsz = x_lat.shape[0]
    s0 = jnp.zeros((bsz, nqk // 128, 2 * GLA_DV, 2 * GLA_DK), F32)
    of_c, ob_c, s_f, s_b = _gla_call(pb_c, lg_c, rope_ctx[0], rope_ctx[1], s0, s0, tb=_tile(lc, 512))
    of, ob, _, _ = _gla_call(pb, lg, rope_lat[0], rope_lat[1], s_f, s_b, tb=_tile(l, 512))

    w_o = w_out.astype(BF16)
    gg = gla_g.reshape(1, -1)
    x_lat = _combine_call(oa, of, ob, pb, gg, w_o, x_lat, g1, _tile(l, 512))
    if need_ctx:
        oa_c = _ctx_attn_call(pa_c, nh)
        x_ctx = _combine_call(oa_c, of_c, ob_c, pb_c, gg, w_o, x_ctx, gc1, _tile(lc, 512))
    return x_lat, x_ctx


def kernel(x, c, ctx, c_ctx, w_mod, b_mod, norm1_g, norm2_g, w_in, w_gate2, b_gate, rpb, gla_norm_g, w_out,
           pool_w, pool_scale, w_up, conv_w, conv_b, w_down, final_g):
    bsz, seq, d = x.shape
    lc = ctx.shape[1]
    depth = w_mod.shape[0]

    cvec = jnp.zeros((8, d), F32).at[0:bsz].set(c).at[bsz].set(c_ctx)
    mods = _mod_call(cvec, w_mod, b_mod)

    rope_lat = _rope_tables(seq)
    rope_ctx = (jnp.ones((lc, 128), F32), jnp.zeros((lc, 128), F32))

    fg = final_g.reshape(1, d)
    x_lat, x_ctx = x, ctx
    for i in range(depth):
        is_even = i % 2 == 0
        need_ctx = any(j % 2 == 0 for j in range(i + 1, depth))
        m = mods[i].reshape(8, 6, d)
        lat = [m[0:bsz, k][:, None, :] for k in range(6)]
        cx = [jnp.broadcast_to(m[bsz, k][None, None, :], (bsz, 1, d)) for k in range(6)]
        n1 = norm1_g[i].reshape(1, d)
        n2 = norm2_g[i].reshape(1, d)
        wu = _col_tiles(w_up[i].astype(BF16), 512)
        wd = w_down[i].astype(BF16)
        cb = conv_b[i].reshape(1, -1)
        if is_even:
            e = i // 2
            x_lat, x_ctx = _even_mixer(x_lat, x_ctx, lat[0:3], cx[0:3], n1, w_in[e], w_gate2[e], b_gate[e], rpb[e],
                                       gla_norm_g[e], w_out[e], need_ctx, rope_lat, rope_ctx)
        else:
            o = i // 2
            pw = pool_w[o].astype(BF16)
            ps = pool_scale[o].reshape(1, d)
            x_lat = _pool_call(x_lat, n1, lat[0], lat[1], pw, ps, lat[2], _tile(seq, 512))
            if need_ctx:
                x_ctx = _pool_call(x_ctx, n1, cx[0], cx[1], pw, ps, cx[2], _tile(lc, 512))
        x_lat = _ffn_call(x_lat, n2, lat[3], lat[4], wu, conv_w[i], cb, wd, lat[5], fg, _tile(seq, 512),
                          final_norm=(i == depth - 1))
        if need_ctx:
            x_ctx = _ffn_call(x_ctx, n2, cx[3], cx[4], wu, conv_w[i], cb, wd, cx[5], fg, _tile(lc, 512),
                              final_norm=False)
    return x_lat
```

```python
import functools

import numpy as np
import jax
import jax.numpy as jnp
from jax import lax
from jax.experimental import pallas as pl
from jax.experimental.pallas import tpu as pltpu

F32 = jnp.float32
BF16 = jnp.bfloat16

GRID_W = 64
HEAD_DIM = 128
NA_ROWS = 8
NA_COLS = 16
NA_GROUP = 4
GLA_DK = 64
GLA_DV = 128
GLA_RANK = 16
GLA_TAU = 16.0
GLA_CHUNK = 64
POOL_WINDOWS = (2, 4, 8, 16)
ROPE_BASE = 10000.0
EPS = 1e-6
HALO = 16
VMEM_LIMIT = 56 * 1024 * 1024


def _cparams(n_axes):
    return pltpu.CompilerParams(dimension_semantics=("arbitrary",) * n_axes,
                                vmem_limit_bytes=VMEM_LIMIT)


def _bdot(a, b):
    return jnp.dot(a, b, preferred_element_type=F32)


def _bdot_nt(a, b):
    return lax.dot_general(a, b, (((1,), (1,)), ((), ())), preferred_element_type=F32)


def _norm_mod(x, g, shift, scale):
    ms = jnp.mean(x * x, axis=-1, keepdims=True)
    return x * lax.rsqrt(ms + EPS) * (g * (1.0 + scale)) + shift


def _mod_kernel(c_ref, w_ref, b_ref, o_ref):
    c = c_ref[...]
    s = c / (1.0 + jnp.exp(-c))
    o_ref[0] = _bdot(s.astype(BF16), w_ref[0].astype(BF16)) + b_ref[0]


def _mod_call(cvec, w_mod, b_mod):
    depth, d, n = w_mod.shape
    tn = 1024
    return pl.pallas_call(
        _mod_kernel,
        out_shape=jax.ShapeDtypeStruct((depth, 8, n), F32),
        grid=(depth, n // tn),
        in_specs=[pl.BlockSpec((8, d), lambda l, j: (0, 0)),
                  pl.BlockSpec((1, d, tn), lambda l, j: (l, 0, j)),
                  pl.BlockSpec((1, 1, tn), lambda l, j: (l, 0, j))],
        out_specs=pl.BlockSpec((1, 8, tn), lambda l, j: (l, 0, j)),
        compiler_params=_cparams(2),
        name="mod_matvec",
    )(cvec, w_mod, b_mod.reshape(depth, 1, n))


def _proj_kernel(x_ref, g_ref, sh_ref, sc_ref, wa_ref, wb_ref, wab_ref, wg2_ref, bg_ref,
                 oa_ref, ob_ref, lg_ref, h_scr, *, na_tiles):
    j = pl.program_id(2)

    @pl.when(j == 0)
    def _():
        h = _norm_mod(x_ref[0], g_ref[...], sh_ref[0], sc_ref[0]).astype(BF16)
        h_scr[...] = h
        ab = _bdot(h, wab_ref[...])
        z = _bdot(ab.astype(BF16), wg2_ref[...]) + bg_ref[...]
        log_sig = jnp.minimum(z, 0.0) - jnp.log1p(jnp.exp(-jnp.abs(z)))
        lg_ref[0] = log_sig * (1.0 / GLA_TAU)

    @pl.when(j < na_tiles)
    def _():
        oa_ref[0] = _bdot(h_scr[...], wa_ref[...]).astype(oa_ref.dtype)

    @pl.when(j >= na_tiles)
    def _():
        ob_ref[0] = _bdot(h_scr[...], wb_ref[...]).astype(ob_ref.dtype)


def _proj_call(x, g, shift, scale, w_a, w_b, wab, wg2, bg, tm, tn):
    b, l, d = x.shape
    na, nb_ = w_a.shape[1], w_b.shape[1]
    ta, tb_ = na // tn, nb_ // tn
    ng = wg2.shape[1]
    kern = functools.partial(_proj_kernel, na_tiles=ta)
    return pl.pallas_call(
        kern,
        out_shape=(jax.ShapeDtypeStruct((b, l, na), BF16), jax.ShapeDtypeStruct((b, l, nb_), BF16),
                   jax.ShapeDtypeStruct((b, l, ng), F32)),
        grid=(b, l // tm, ta + tb_),
        in_specs=[pl.BlockSpec((1, tm, d), lambda bi, i, j: (bi, i, 0)),
                  pl.BlockSpec((1, d), lambda bi, i, j: (0, 0)),
                  pl.BlockSpec((1, 1, d), lambda bi, i, j: (bi, 0, 0)),
                  pl.BlockSpec((1, 1, d), lambda bi, i, j: (bi, 0, 0)),
                  pl.BlockSpec((d, tn), lambda bi, i, j: (0, jnp.minimum(j, ta - 1))),
                  pl.BlockSpec((d, tn), lambda bi, i, j: (0, jnp.maximum(j - ta, 0))),
                  pl.BlockSpec(wab.shape, lambda bi, i, j: (0, 0)),
                  pl.BlockSpec(wg2.shape, lambda bi, i, j: (0, 0)),
                  pl.BlockSpec((1, ng), lambda bi, i, j: (0, 0))],
        out_specs=(pl.BlockSpec((1, tm, tn), lambda bi, i, j: (bi, i, jnp.minimum(j, ta - 1))),
                   pl.BlockSpec((1, tm, tn), lambda bi, i, j: (bi, i, jnp.maximum(j - ta, 0))),
                   pl.BlockSpec((1, tm, ng), lambda bi, i, j: (bi, i, 0))),
        scratch_shapes=[pltpu.VMEM((tm, d), BF16)],
        compiler_params=_cparams(3),
        name="proj_in",
    )(x, g, shift, scale, w_a, w_b, wab, wg2, bg)


def _na_kernel(rpb_ref, q_ref, k_ref, v_ref, kc_ref, vc_ref, o_ref, tab_ref, *, rows, n_dr, n_dc):
    head = pl.program_id(0)
    scale = HEAD_DIM ** -0.5
    wr = NA_ROWS

    qc = lax.broadcasted_iota(jnp.int32, (GRID_W, 2 * GRID_W), 0)
    lane = lax.broadcasted_iota(jnp.int32, (GRID_W, 2 * GRID_W), 1)
    kc = lane & (GRID_W - 1)
    dc = jnp.clip(kc - qc + (NA_COLS - 1), 0, n_dc - 1)
    cstart = jnp.clip(qc - NA_COLS // 2, 0, GRID_W - NA_COLS)
    in_win = (kc >= cstart) & (kc < cstart + NA_COLS)
    left = lane < GRID_W
    base_off = head * (n_dr * n_dc)

    @pl.when(pl.program_id(1) == 0)
    def _():
        for d in range(-1, n_dr):
            acc = jnp.zeros((GRID_W, 2 * GRID_W), F32)
            for c in range(n_dc):
                if d < 0:
                    val = rpb_ref[base_off + (d + 1) * n_dc + c]
                elif d + 1 >= n_dr:
                    val = rpb_ref[base_off + d * n_dc + c]
                else:
                    val = jnp.where(left, rpb_ref[base_off + d * n_dc + c], rpb_ref[base_off + (d + 1) * n_dc + c])
                acc = jnp.where(dc == c, val, acc)
            keep = in_win & ~left if d < 0 else (in_win & left if d + 1 >= n_dr else in_win)
            tab_ref[d + 1] = jnp.where(keep, acc, -jnp.inf)

    kctx = kc_ref[0]
    vctx = vc_ref[0]
    grp = NA_GROUP
    union = grp + wr
    neg_inf = jnp.full((GRID_W, 2 * GRID_W), -jnp.inf, F32)

    def row_group(gi, carry):
        r0 = gi * grp
        ustart = jnp.clip(r0 - wr // 2, 0, rows - union)
        q_rows = pl.ds(pl.multiple_of(r0 * GRID_W, grp * GRID_W), grp * GRID_W)
        kv_rows = pl.ds(pl.multiple_of(ustart * GRID_W, GRID_W), union * GRID_W)
        q = q_ref[0, q_rows, :]
        kb = k_ref[0, kv_rows, :]
        vb = v_ref[0, kv_rows, :]
        biases = []
        for g in range(grp):
            rq = r0 + g
            rstart = jnp.clip(rq - wr // 2, 0, rows - wr)
            tiles = []
            for j in range(union // 2):
                ku = ustart + 2 * j
                d = ku - rq + (NA_ROWS - 1)
                ok_l = ((ku >= rstart) & (ku < rstart + wr)).astype(jnp.int32)
                ok_r = ((ku + 1 >= rstart) & (ku + 1 < rstart + wr)).astype(jnp.int32)
                tile = tab_ref[jnp.clip(d, -1, n_dr - 1) + 1]
                tiles.append(jnp.where(jnp.where(left, ok_l, ok_r) > 0, tile, neg_inf))
            biases.append(jnp.concatenate(tiles, axis=1))
        s_loc = _bdot_nt(q, kb) * scale + jnp.concatenate(biases, axis=0)
        s_ctx = _bdot_nt(q, kctx) * scale
        m = jnp.maximum(jnp.max(s_loc, axis=-1, keepdims=True), jnp.max(s_ctx, axis=-1, keepdims=True))
        p_loc = jnp.exp(s_loc - m)
        p_ctx = jnp.exp(s_ctx - m)
        denom = jnp.sum(p_loc, axis=-1, keepdims=True) + jnp.sum(p_ctx, axis=-1, keepdims=True)
        o = _bdot(p_loc.astype(BF16), vb) + _bdot(p_ctx.astype(BF16), vctx)
        o_ref[0, q_rows, :] = (o / denom).astype(o_ref.dtype)
        return carry

    lax.fori_loop(0, rows // grp, row_group, 0, unroll=2)


def _na_call(pa, pa_ctx, rpb):
    b, l, _ = pa.shape
    lc = pa_ctx.shape[1]
    nh, n_dr, n_dc = rpb.shape
    rows = l // GRID_W
    kern = functools.partial(_na_kernel, rows=rows, n_dr=n_dr, n_dc=n_dc)
    return pl.pallas_call(
        kern,
        out_shape=jax.ShapeDtypeStruct((b, l, nh * HEAD_DIM), BF16),
        grid=(nh, b),
        in_specs=[pl.BlockSpec(memory_space=pltpu.SMEM),
                  pl.BlockSpec((1, l, HEAD_DIM), lambda h, bi: (bi, 0, h)),
                  pl.BlockSpec((1, l, HEAD_DIM), lambda h, bi: (bi, 0, nh + h)),
                  pl.BlockSpec((1, l, HEAD_DIM), lambda h, bi: (bi, 0, 2 * nh + h)),
                  pl.BlockSpec((1, lc, HEAD_DIM), lambda h, bi: (bi, 0, nh + h)),
                  pl.BlockSpec((1, lc, HEAD_DIM), lambda h, bi: (bi, 0, 2 * nh + h))],
        out_specs=pl.BlockSpec((1, l, HEAD_DIM), lambda h, bi: (bi, 0, h)),
        scratch_shapes=[pltpu.VMEM((n_dr + 1, GRID_W, 2 * GRID_W), F32)],
        compiler_params=_cparams(2),
        name="na_attention",
    )(rpb.reshape(-1), pa, pa, pa, pa_ctx, pa_ctx)


def _ctx_attn_kernel(q_ref, k_ref, v_ref, o_ref):
    q = q_ref[0]
    s = _bdot_nt(q, k_ref[0]) * (HEAD_DIM ** -0.5)
    m = jnp.max(s, axis=-1, keepdims=True)
    p = jnp.exp(s - m)
    denom = jnp.sum(p, axis=-1, keepdims=True)
    o_ref[0] = (_bdot(p.astype(BF16), v_ref[0]) / denom).astype(o_ref.dtype)


def _ctx_attn_call(pa_ctx, nh):
    b, lc, _ = pa_ctx.shape
    return pl.pallas_call(
        _ctx_attn_kernel,
        out_shape=jax.ShapeDtypeStruct((b, lc, nh * HEAD_DIM), BF16),
        grid=(b, nh),
        in_specs=[pl.BlockSpec((1, lc, HEAD_DIM), lambda bi, h: (bi, 0, h)),
                  pl.BlockSpec((1, lc, HEAD_DIM), lambda bi, h: (bi, 0, nh + h)),
                  pl.BlockSpec((1, lc, HEAD_DIM), lambda bi, h: (bi, 0, 2 * nh + h))],
        out_specs=pl.BlockSpec((1, lc, HEAD_DIM), lambda bi, h: (bi, 0, h)),
        compiler_params=_cparams(2),
        name="ctx_attention",
    )(pa_ctx, pa_ctx, pa_ctx)


_GLA_LEVELS = (32, 16, 8, 4, 2, 1)


def _gla_dir_consts(rev):
    c = GLA_CHUNK
    ii = lax.broadcasted_iota(jnp.int32, (c, c), 0)
    jj = lax.broadcasted_iota(jnp.int32, (c, c), 1)
    tri = ((jj >= ii) if rev else (jj <= ii)).astype(BF16)
    i2 = lax.broadcasted_iota(jnp.int32, (c, 2 * c), 0)
    j2 = lax.broadcasted_iota(jnp.int32, (c, 2 * c), 1) & (c - 1)
    masks = [i2 == j2]
    for s in _GLA_LEVELS:
        same = (i2 // (2 * s)) == (j2 // (2 * s))
        qi = i2 % (2 * s)
        kj = j2 % (2 * s)
        masks.append(same & ((qi < s) & (kj >= s) if rev else (qi >= s) & (kj < s)))
    return tri, masks


def _gla_ref_rows(cum, s, rev, sub8):
    c, w = cum.shape
    blk = 2 * s
    off = s - 1 if rev else s
    pieces = []
    if blk >= 8:
        for b0 in range(0, c, blk):
            pieces.append(jnp.broadcast_to(cum[b0 + off:b0 + off + 1, :], (blk, w)))
    else:
        for g0 in range(0, c, 8):
            out = None
            for b0 in range(0, 8, blk):
                cand = jnp.broadcast_to(cum[g0 + b0 + off:g0 + b0 + off + 1, :], (8, w))
                out = cand if out is None else jnp.where(sub8 >= b0, cand, out)
            pieces.append(out)
    return jnp.concatenate(pieces, axis=0)


def _gla_prep(rev, tri, sub8, q, k, g):
    c = GLA_CHUNK
    g1 = g.astype(BF16)
    r1 = g - g1.astype(F32)
    g2 = r1.astype(BF16)
    g3 = (r1 - g2.astype(F32)).astype(BF16)
    cum = _bdot(tri, g1) + _bdot(tri, g2) + _bdot(tri, g3)
    end_row = 0 if rev else c - 1
    last = cum[end_row:end_row + 1, :]
    q_in = (q * jnp.exp(cum)).astype(BF16)
    k_out = (k * jnp.exp(last - cum)).astype(BF16)
    e_last = jnp.exp(last)

    qs = [q.astype(BF16)]
    ks = [k.astype(BF16)]
    for s in _GLA_LEVELS:
        dq = cum - _gla_ref_rows(cum, s, rev, sub8)
        e = jnp.exp(jnp.minimum(dq, -dq))
        qs.append((q * e).astype(BF16))
        ks.append((k * e).astype(BF16))
    return qs, ks, q_in, k_out, e_last


def _gla_scores(masks, left128, qs, ks, n_pairs):
    c = GLA_CHUNK
    attns = [jnp.zeros((c, 2 * c), F32) for _ in range(n_pairs)]
    for lv_i in range(len(masks)):
        for p in range(n_pairs):
            ksl = slice(p * 128, (p + 1) * 128)
            kp = ks[lv_i][:, ksl]
            kbd = jnp.concatenate([jnp.where(left128, kp, 0), jnp.where(left128, 0, kp)], axis=0)
            attns[p] = jnp.where(masks[lv_i], _bdot_nt(qs[lv_i][:, ksl], kbd), attns[p])
    return [a.astype(BF16) for a in attns]


def _gla_finish(left256, st_diag, attns, q_in, k_out, e_last, v, st_ref, n_pairs):
    outs = []
    for p in range(n_pairs):
        ksl = slice(p * 128, (p + 1) * 128)
        vp16 = v[:, p * 256:(p + 1) * 256]
        vbd = jnp.concatenate([jnp.where(left256, vp16, 0), jnp.where(left256, 0, vp16)], axis=0)
        st = st_ref[p]
        outs.append(_bdot(attns[p], vbd) + _bdot_nt(q_in[:, ksl], st.astype(BF16)))
        upd = lax.dot_general(vp16, k_out[:, ksl], (((0,), (0,)), ((), ())), preferred_element_type=F32)
        st_ref[p] = st * e_last[:, ksl] + jnp.where(st_diag, upd, 0.0)
    return jnp.concatenate(outs, axis=1).astype(BF16)


def _gla_kernel(qf_ref, kf_ref, vf_ref, gf_ref, cosf_ref, sinf_ref,
                qr_ref, kr_ref, vr_ref, gr_ref, cosr_ref, sinr_ref, s0f_ref, s0r_ref,
                of_ref, or_ref, sff_ref, sfr_ref, stf_scr, str_scr, *, n_chunks, n_pairs):
    c = GLA_CHUNK
    blk_i = pl.program_id(1)

    @pl.when(blk_i == 0)
    def _():
        stf_scr[...] = s0f_ref[0]
        str_scr[...] = s0r_ref[0]

    l2 = lax.broadcasted_iota(jnp.int32, (c, 2 * c), 1)
    left128 = l2 < c
    lane32 = l2 & 31
    left256 = lax.broadcasted_iota(jnp.int32, (c, 2 * GLA_DV), 1) < GLA_DV
    sr = lax.broadcasted_iota(jnp.int32, (2 * GLA_DV, 2 * GLA_DK), 0)
    sc_ = lax.broadcasted_iota(jnp.int32, (2 * GLA_DV, 2 * GLA_DK), 1)
    st_diag = (sr // GLA_DV) == (sc_ // GLA_DK)
    sub8 = lax.broadcasted_iota(jnp.int32, (8, n_pairs * 128), 0)
    tri_f, masks_f = _gla_dir_consts(False)
    tri_r, masks_r = _gla_dir_consts(True)

    def rope(x, cos, sin):
        outs = []
        for p in range(n_pairs):
            xs = x[:, p * 128:(p + 1) * 128]
            up = pltpu.roll(xs, 16, axis=1)
            dn = pltpu.roll(xs, 112, axis=1)
            sw = jnp.where(lane32 < 16, dn, up)
            outs.append(xs * cos + sw * sin)
        return jnp.concatenate(outs, axis=1)

    def prep(rev, tri, rows, q_ref, k_ref, g_ref, cos_ref, sin_ref):
        cos = cos_ref[rows, :]
        sin = sin_ref[rows, :]
        q = rope(q_ref[0, rows, :].astype(F32), cos, sin) * (GLA_DK ** -0.5)
        k = rope(k_ref[0, rows, :].astype(F32), cos, sin)
        return _gla_prep(rev, tri, sub8, q, k, g_ref[0, rows, :])

    def chunk(ci, carry):
        rows_f = pl.ds(pl.multiple_of(ci * c, c), c)
        rows_r = pl.ds(pl.multiple_of((n_chunks - 1 - ci) * c, c), c)
        qs_f, ks_f, qin_f, kout_f, el_f = prep(False, tri_f, rows_f, qf_ref, kf_ref, gf_ref, cosf_ref, sinf_ref)
        qs_r, ks_r, qin_r, kout_r, el_r = prep(True, tri_r, rows_r, qr_ref, kr_ref, gr_ref, cosr_ref, sinr_ref)
        at_f = _gla_scores(masks_f, left128, qs_f, ks_f, n_pairs)
        at_r = _gla_scores(masks_r, left128, qs_r, ks_r, n_pairs)
        of_ref[0, rows_f, :] = _gla_finish(left256, st_diag, at_f, qin_f, kout_f, el_f, vf_ref[0, rows_f, :],
                                           stf_scr, n_pairs)
        or_ref[0, rows_r, :] = _gla_finish(left256, st_diag, at_r, qin_r, kout_r, el_r, vr_ref[0, rows_r, :],
                                           str_scr, n_pairs)
        return carry

    lax.fori_loop(0, n_chunks, chunk, 0)

    @pl.when(blk_i == pl.num_programs(1) - 1)
    def _():
        sff_ref[0] = stf_scr[...]
        sfr_ref[0] = str_scr[...]


def _gla_call(pb, lg, cos, sin, s0f, s0r, *, tb):
    b, l, _ = pb.shape
    nqk = lg.shape[2] // 2
    n_pairs = nqk // 128
    nv = n_pairs * 2 * GLA_DV
    nb = l // tb
    kern = functools.partial(_gla_kernel, n_chunks=tb // GLA_CHUNK, n_pairs=n_pairs)
    st_shape = (n_pairs, 2 * GLA_DV, 2 * GLA_DK)

    def dir_specs(tok, gate_blk):
        return [pl.BlockSpec((1, tb, nqk), lambda bi, i: (bi, tok(i), 0)),
                pl.BlockSpec((1, tb, nqk), lambda bi, i: (bi, tok(i), 1)),
                pl.BlockSpec((1, tb, nv), lambda bi, i: (bi, tok(i), 1)),
                pl.BlockSpec((1, tb, nqk), lambda bi, i: (bi, tok(i), gate_blk)),
                pl.BlockSpec((tb, 128), lambda bi, i: (tok(i), 0)),
                pl.BlockSpec((tb, 128), lambda bi, i: (tok(i), 0))]

    fwd_tok = lambda i: i
    rev_tok = lambda i: nb - 1 - i
    st_spec = pl.BlockSpec((1,) + st_shape, lambda bi, i: (bi, 0, 0, 0))
    return pl.pallas_call(
        kern,
        out_shape=(jax.ShapeDtypeStruct((b, l, nv), BF16), jax.ShapeDtypeStruct((b, l, nv), BF16),
                   jax.ShapeDtypeStruct((b,) + st_shape, F32), jax.ShapeDtypeStruct((b,) + st_shape, F32)),
        grid=(b, nb),
        in_specs=dir_specs(fwd_tok, 0) + dir_specs(rev_tok, 1) + [st_spec, st_spec],
        out_specs=(pl.BlockSpec((1, tb, nv), lambda bi, i: (bi, fwd_tok(i), 0)),
                   pl.BlockSpec((1, tb, nv), lambda bi, i: (bi, rev_tok(i), 0)),
                   st_spec, st_spec),
        scratch_shapes=[pltpu.VMEM(st_shape, F32), pltpu.VMEM(st_shape, F32)],
        compiler_params=_cparams(2),
        name="gla_scan",
    )(pb, pb, pb, lg, cos, sin, pb, pb, pb, lg, cos, sin, s0f, s0r)


def _rope_tables(seq_len):
    t = jnp.arange(seq_len)
    row = (t // GRID_W).astype(F32)
    col = (t % GRID_W).astype(F32)
    nf = GLA_DK // 4
    inv = ROPE_BASE ** (-jnp.arange(nf, dtype=F32) / nf)
    ar = row[:, None] * inv[None, :]
    ac = col[:, None] * inv[None, :]
    cos = jnp.concatenate([jnp.cos(ar), jnp.cos(ar), jnp.cos(ac), jnp.cos(ac)], axis=1)
    sin = jnp.concatenate([-jnp.sin(ar), jnp.sin(ar), -jnp.sin(ac), jnp.sin(ac)], axis=1)
    return jnp.tile(cos, (1, 2)), jnp.tile(sin, (1, 2))


def _combine_kernel(oa_ref, of_ref, ob_ref, gb_ref, gg_ref, w_ref, x_ref, g1_ref, o_ref, *, n_heads):
    ob = of_ref[0].astype(F32) + ob_ref[0].astype(F32)
    parts = []
    for h in range(n_heads):
        oh = ob[:, h * GLA_DV:(h + 1) * GLA_DV]
        ms = jnp.mean(oh * oh, axis=-1, keepdims=True)
        parts.append(oh * lax.rsqrt(ms + EPS))
    obn = jnp.concatenate(parts, axis=1) * gg_ref[...]
    gb = gb_ref[0].astype(F32)
    yb = obn * (gb / (1.0 + jnp.exp(-gb)))
    na_w = oa_ref.shape[2]
    y = _bdot(oa_ref[0], w_ref[0:na_w, :]) + _bdot(yb.astype(BF16), w_ref[na_w:, :])
    o_ref[0] = x_ref[0] + g1_ref[0] * y


def _combine_call(oa, of, ob, pb, gla_g, w_out, x, g1, tm):
    b, l, d = x.shape
    na_w = oa.shape[2]
    gl_w = of.shape[2]
    n_heads = gl_w // GLA_DV
    gate_blk = (pb.shape[2] - gl_w) // gl_w
    kern = functools.partial(_combine_kernel, n_heads=n_heads)
    return pl.pallas_call(
        kern,
        out_shape=jax.ShapeDtypeStruct((b, l, d), F32),
        grid=(b, l // tm),
        in_specs=[pl.BlockSpec((1, tm, na_w), lambda bi, i: (bi, i, 0)),
                  pl.BlockSpec((1, tm, gl_w), lambda bi, i: (bi, i, 0)),
                  pl.BlockSpec((1, tm, gl_w), lambda bi, i: (bi, i, 0)),
                  pl.BlockSpec((1, tm, gl_w), lambda bi, i: (bi, i, gate_blk)),
                  pl.BlockSpec((1, gl_w), lambda bi, i: (0, 0)),
                  pl.BlockSpec(w_out.shape, lambda bi, i: (0, 0)),
                  pl.BlockSpec((1, tm, d), lambda bi, i: (bi, i, 0)),
                  pl.BlockSpec((1, 1, d), lambda bi, i: (bi, 0, 0))],
        out_specs=pl.BlockSpec((1, tm, d), lambda bi, i: (bi, i, 0)),
        compiler_params=_cparams(2),
        name="combine_out",
    )(oa, of, ob, pb, gla_g, w_out, x, g1)


def _pool_kernel(x_ref, xp_ref, xn_ref, g_ref, sh_ref, sc_ref, w_ref, ps_ref, g1_ref, o_ref, h_scr,
                 *, tm, seq_len):
    i = pl.program_id(1)
    n_i = pl.num_programs(1)
    g = g_ref[...]
    sh = sh_ref[0]
    sc = sc_ref[0]
    x = x_ref[0]
    hm = _norm_mod(x, g, sh, sc)
    h_scr[HALO:HALO + tm, :] = hm
    h_scr[0:HALO, :] = jnp.where(i > 0, _norm_mod(xp_ref[0], g, sh, sc), 0.0)
    h_scr[HALO + tm:, :] = jnp.where(i < n_i - 1, _norm_mod(xn_ref[0], g, sh, sc), 0.0)

    t = i * tm + lax.broadcasted_iota(jnp.int32, (tm, 1), 0)
    n_ext = tm + 2 * HALO
    grp = hm.shape[1] // len(POOL_WINDOWS)
    ys = []
    for gi, w in enumerate(POOL_WINDOWS):
        parts = []
        for c0 in range(gi * grp, (gi + 1) * grp, 128):
            e = h_scr[:, c0:c0 + 128]
            acc = e + pltpu.roll(e, 1, axis=0)
            n = 2
            while n < w:
                acc = pltpu.roll(acc, n // 2, axis=0) + pltpu.roll(acc, n_ext - n // 2, axis=0)
                n *= 2
            parts.append(acc[HALO:HALO + tm])
        cols = slice(gi * grp, (gi + 1) * grp)
        cnt = (jnp.minimum(t + w // 2, seq_len) - jnp.maximum(t - w // 2, 0)).astype(F32)
        pooled = jnp.concatenate(parts, axis=1) / cnt - hm[:, cols]
        ys.append(_bdot(pooled.astype(BF16), w_ref[gi]))
    y = jnp.concatenate(ys, axis=1) * ps_ref[...]
    o_ref[0] = x + g1_ref[0] * y


def _halo_specs(tm, l, d):
    per = tm // HALO
    last = l // HALO - 1
    prev = pl.BlockSpec((1, HALO, d), lambda bi, i, *_: (bi, jnp.maximum(i * per - 1, 0), 0))
    nxt = pl.BlockSpec((1, HALO, d), lambda bi, i, *_: (bi, jnp.minimum((i + 1) * per, last), 0))
    return prev, nxt


def _pool_call(x, g, shift, scale, pool_w, pool_scale, g1, tm):
    b, l, d = x.shape
    prev, nxt = _halo_specs(tm, l, d)
    kern = functools.partial(_pool_kernel, tm=tm, seq_len=l)
    return pl.pallas_call(
        kern,
        out_shape=jax.ShapeDtypeStruct((b, l, d), F32),
        grid=(b, l // tm),
        in_specs=[pl.BlockSpec((1, tm, d), lambda bi, i: (bi, i, 0)), prev, nxt,
                  pl.BlockSpec((1, d), lambda bi, i: (0, 0)),
                  pl.BlockSpec((1, 1, d), lambda bi, i: (bi, 0, 0)),
                  pl.BlockSpec((1, 1, d), lambda bi, i: (bi, 0, 0)),
                  pl.BlockSpec(pool_w.shape, lambda bi, i: (0, 0, 0)),
                  pl.BlockSpec((1, d), lambda bi, i: (0, 0)),
                  pl.BlockSpec((1, 1, d), lambda bi, i: (bi, 0, 0))],
        out_specs=pl.BlockSpec((1, tm, d), lambda bi, i: (bi, i, 0)),
        scratch_shapes=[pltpu.VMEM((tm + 2 * HALO, d), F32)],
        compiler_params=_cparams(2),
        name="pool_mixer",
    )(x, x, x, g, shift, scale, pool_w, pool_scale, g1)


def _ffn_kernel(x_ref, xp_ref, xn_ref, g_ref, sh_ref, sc_ref, wv_ref, wg_ref, cw_ref, cb_ref, wd_ref,
                g2_ref, fg_ref, o_ref, h_scr, acc_scr, *, tm, seg, final_norm):
    i = pl.program_id(1)
    j = pl.program_id(2)

    @pl.when(j == 0)
    def _():
        g = g_ref[...]
        sh = sh_ref[0]
        sc = sc_ref[0]
        h_scr[HALO:HALO + tm, :] = _norm_mod(x_ref[0], g, sh, sc).astype(BF16)
        h_scr[0:HALO, :] = _norm_mod(xp_ref[0], g, sh, sc).astype(BF16)
        h_scr[HALO + tm:, :] = _norm_mod(xn_ref[0], g, sh, sc).astype(BF16)
        acc_scr[...] = jnp.zeros_like(acc_scr)

    rows_all = tm + 2 * HALO
    ug = _bdot(h_scr[...], wg_ref[...])
    uv = _bdot(h_scr[HALO:HALO + tm, :], wv_ref[...])
    cw = cw_ref[...]
    pos = lax.rem(i * tm + lax.broadcasted_iota(jnp.int32, (tm, 1), 0), seg)
    g_prev = jnp.where(pos == 0, 0.0, pltpu.roll(ug, 1, axis=0)[HALO:HALO + tm])
    g_next = jnp.where(pos == seg - 1, 0.0, pltpu.roll(ug, rows_all - 1, axis=0)[HALO:HALO + tm])
    gate = g_prev * cw[0:1] + ug[HALO:HALO + tm] * cw[1:2] + g_next * cw[2:3] + cb_ref[...]
    act = 0.5 * gate * (1.0 + lax.erf(gate * (2.0 ** -0.5))) * uv
    acc_scr[...] += _bdot(act.astype(BF16), wd_ref[...])

    @pl.when(j == pl.num_programs(2) - 1)
    def _():
        y = x_ref[0] + g2_ref[0] * acc_scr[...]
        if final_norm:
            ms = jnp.mean(y * y, axis=-1, keepdims=True)
            y = y * lax.rsqrt(ms + EPS) * fg_ref[...]
        o_ref[0] = y


def _ffn_call(x, g, shift, scale, w_up, conv_w, conv_b, w_down, g2, final_g, tm, tf, seg, final_norm):
    b, l, d = x.shape
    nf = w_down.shape[0] // tf
    prev, nxt = _halo_specs(tm, l, d)
    kern = functools.partial(_ffn_kernel, tm=tm, seg=seg, final_norm=final_norm)
    return pl.pallas_call(
        kern,
        out_shape=jax.ShapeDtypeStruct((b, l, d), F32),
        grid=(b, l // tm, nf),
        in_specs=[pl.BlockSpec((1, tm, d), lambda bi, i, j: (bi, i, 0)), prev, nxt,
                  pl.BlockSpec((1, d), lambda bi, i, j: (0, 0)),
                  pl.BlockSpec((1, 1, d), lambda bi, i, j: (bi, 0, 0)),
                  pl.BlockSpec((1, 1, d), lambda bi, i, j: (bi, 0, 0)),
                  pl.BlockSpec((d, tf), lambda bi, i, j: (0, j)),
                  pl.BlockSpec((d, tf), lambda bi, i, j: (0, nf + j)),
                  pl.BlockSpec((3, tf), lambda bi, i, j: (0, j)),
                  pl.BlockSpec((1, tf), lambda bi, i, j: (0, j)),
                  pl.BlockSpec((tf, d), lambda bi, i, j: (j, 0)),
                  pl.BlockSpec((1, 1, d), lambda bi, i, j: (bi, 0, 0)),
                  pl.BlockSpec((1, d), lambda bi, i, j: (0, 0))],
        out_specs=pl.BlockSpec((1, tm, d), lambda bi, i, j: (bi, i, 0)),
        scratch_shapes=[pltpu.VMEM((tm + 2 * HALO, d), BF16), pltpu.VMEM((tm, d), F32)],
        compiler_params=_cparams(3),
        name="conv_ffn",
    )(x, x, x, g, shift, scale, w_up, w_up, conv_w, conv_b, w_down, g2, final_g)


def _tile(l, want):
    return min(l, want)


def _even_mixer(x_lat, x_ctx, mods_lat, mods_ctx, n1, w_in, w_gate2, b_gate, rpb, gla_g, w_out, need_ctx,
                rope_lat, rope_ctx):
    sh, sc, g1 = mods_lat
    shc, scc, gc1 = mods_ctx
    nh = rpb.shape[0]
    a_w = 3 * nh * HEAD_DIM
    n_gla = w_gate2.shape[2] // GLA_DK
    b_w = 2 * n_gla * GLA_DK + 2 * n_gla * GLA_DV
    w_a = w_in[:, :a_w].astype(BF16)
    w_b = w_in[:, a_w:a_w + b_w].astype(BF16)
    w_ab = jnp.pad(w_in[:, a_w + b_w:], ((0, 0), (0, 128 - 2 * GLA_RANK))).astype(BF16)
    nqk = n_gla * GLA_DK
    wg2 = jnp.zeros((128, 2 * nqk), F32)
    wg2 = wg2.at[0:GLA_RANK, 0:nqk].set(w_gate2[0]).at[GLA_RANK:2 * GLA_RANK, nqk:].set(w_gate2[1]).astype(BF16)
    bg = b_gate.reshape(1, 2 * nqk)

    l = x_lat.shape[1]
    lc = x_ctx.shape[1]
    bsz, d = x_lat.shape[0], x_lat.shape[2]
    pa, pb, lg = _proj_call(x_lat, n1, sh, sc, w_a, w_b, w_ab, wg2, bg, _tile(l, 1024), 512)
    ctx_flat = x_ctx.reshape(1, bsz * lc, d)
    pa_c, pb_c, lg_c = [a.reshape(bsz, lc, -1) for a in
                        _proj_call(ctx_flat, n1, shc[:1], scc[:1], w_a, w_b, w_ab, wg2, bg, _tile(bsz * lc, 1024), 512)]

    oa = _na_call(pa, pa_c, rpb)

    s0 = jnp.zeros((bsz, nqk // 128, 2 * GLA_DV, 2 * GLA_DK), F32)
    of_c, ob_c, s_f, s_b = _gla_call(pb_c, lg_c, rope_ctx[0], rope_ctx[1], s0, s0, tb=_tile(lc, 512))
    of, ob, _, _ = _gla_call(pb, lg, rope_lat[0], rope_lat[1], s_f, s_b, tb=_tile(l, 512))

    w_o = w_out.astype(BF16)
    gg = gla_g.reshape(1, -1)
    x_lat = _combine_call(oa, of, ob, pb, gg, w_o, x_lat, g1, _tile(l, 512))
    if need_ctx:
        oa_c = _ctx_attn_call(pa_c, nh)
        x_ctx = _combine_call(oa_c, of_c, ob_c, pb_c, gg, w_o, x_ctx, gc1, _tile(lc, 512))
    return x_lat, x_ctx


def kernel(x, c, ctx, c_ctx, w_mod, b_mod, norm1_g, norm2_g, w_in, w_gate2, b_gate, rpb, gla_norm_g, w_out,
           pool_w, pool_scale, w_up, conv_w, conv_b, w_down, final_g):
    bsz, seq, d = x.shape
    lc = ctx.shape[1]
    depth = w_mod.shape[0]

    cvec = jnp.zeros((8, d), F32).at[0:bsz].set(c).at[bsz].set(c_ctx)
    mods = _mod_call(cvec, w_mod, b_mod)

    rope_lat = _rope_tables(seq)
    rope_ctx = (jnp.ones((lc, 128), F32), jnp.zeros((lc, 128), F32))

    fg = final_g.reshape(1, d)
    x_lat, x_ctx = x, ctx
    for i in range(depth):
        is_even = i % 2 == 0
        need_ctx = any(j % 2 == 0 for j in range(i + 1, depth))
        m = mods[i].reshape(8, 6, d)
        lat = [m[0:bsz, k][:, None, :] for k in range(6)]
        cx = [jnp.broadcast_to(m[bsz, k][None, None, :], (bsz, 1, d)) for k in range(6)]
        n1 = norm1_g[i].reshape(1, d)
        n2 = norm2_g[i].reshape(1, d)
        wu = w_up[i].astype(BF16)
        wd = w_down[i].astype(BF16)
        cb = conv_b[i].reshape(1, -1)
        if is_even:
            e = i // 2
            x_lat, x_ctx = _even_mixer(x_lat, x_ctx, lat[0:3], cx[0:3], n1, w_in[e], w_gate2[e], b_gate[e], rpb[e],
                                       gla_norm_g[e], w_out[e], need_ctx, rope_lat, rope_ctx)
        else:
            o = i // 2
            pw = pool_w[o].astype(BF16)
            ps = pool_scale[o].reshape(1, d)
            x_lat = _pool_call(x_lat, n1, lat[0], lat[1], pw, ps, lat[2], _tile(seq, 512))
            if need_ctx:
                x_ctx = _pool_call(x_ctx, n1, cx[0], cx[1], pw, ps, cx[2], _tile(lc, 512))
        x_lat = _ffn_call(x_lat, n2, lat[3], lat[4], wu, conv_w[i], cb, wd, lat[5], fg, _tile(seq, 512), 512,
                          seg=seq, final_norm=(i == depth - 1))
        if need_ctx:
            ctx_flat = x_ctx.reshape(1, bsz * lc, d)
            x_ctx = _ffn_call(ctx_flat, n2, cx[3][:1], cx[4][:1], wu, conv_w[i], cb, wd, cx[5][:1], fg,
                              _tile(bsz * lc, 512), 512, seg=lc, final_norm=False).reshape(bsz, lc, d)
    return x_lat
```

```python
import functools

import numpy as np
import jax
import jax.numpy as jnp
from jax import lax
from jax.experimental import pallas as pl
from jax.experimental.pallas import tpu as pltpu

F32 = jnp.float32
BF16 = jnp.bfloat16

GRID_W = 64
HEAD_DIM = 128
NA_ROWS = 8
NA_COLS = 16
NA_GROUP = 4
GLA_DK = 64
GLA_DV = 128
GLA_RANK = 16
GLA_TAU = 16.0
GLA_CHUNK = 64
POOL_WINDOWS = (2, 4, 8, 16)
ROPE_BASE = 10000.0
EPS = 1e-6
LOG2E = 1.4426950408889634
HALO = 16
VMEM_LIMIT = 56 * 1024 * 1024


def _cparams(n_axes):
    return pltpu.CompilerParams(dimension_semantics=("arbitrary",) * n_axes,
                                vmem_limit_bytes=VMEM_LIMIT)


def _bdot(a, b):
    return jnp.dot(a, b, preferred_element_type=F32)


def _bdot_nt(a, b):
    return lax.dot_general(a, b, (((1,), (1,)), ((), ())), preferred_element_type=F32)


def _norm_mod(x, g, shift, scale):
    ms = jnp.mean(x * x, axis=-1, keepdims=True)
    return x * lax.rsqrt(ms + EPS) * (g * (1.0 + scale)) + shift


def _mod_kernel(c_ref, w_ref, b_ref, o_ref):
    c = c_ref[...]
    s = c / (1.0 + jnp.exp(-c))
    o_ref[0] = _bdot(s.astype(BF16), w_ref[0].astype(BF16)) + b_ref[0]


def _mod_call(cvec, w_mod, b_mod):
    depth, d, n = w_mod.shape
    tn = 1024
    return pl.pallas_call(
        _mod_kernel,
        out_shape=jax.ShapeDtypeStruct((depth, 8, n), F32),
        grid=(depth, n // tn),
        in_specs=[pl.BlockSpec((8, d), lambda l, j: (0, 0)),
                  pl.BlockSpec((1, d, tn), lambda l, j: (l, 0, j)),
                  pl.BlockSpec((1, 1, tn), lambda l, j: (l, 0, j))],
        out_specs=pl.BlockSpec((1, 8, tn), lambda l, j: (l, 0, j)),
        compiler_params=_cparams(2),
        name="mod_matvec",
    )(cvec, w_mod, b_mod.reshape(depth, 1, n))


def _proj_kernel(x_ref, g_ref, sh_ref, sc_ref, wa_ref, wb_ref, wab_ref, wg2_ref, bg_ref,
                 oa_ref, ob_ref, lg_ref, h_scr, *, na_tiles):
    j = pl.program_id(2)

    @pl.when(j == 0)
    def _():
        h = _norm_mod(x_ref[0], g_ref[...], sh_ref[0], sc_ref[0]).astype(BF16)
        h_scr[...] = h
        ab = _bdot(h, wab_ref[...])
        z = _bdot(ab.astype(BF16), wg2_ref[...]) + bg_ref[...]
        log_sig = jnp.minimum(z, 0.0) - jnp.log1p(jnp.exp(-jnp.abs(z)))
        lg_ref[0] = log_sig * (1.0 / GLA_TAU)

    @pl.when(j < na_tiles)
    def _():
        oa_ref[0] = _bdot(h_scr[...], wa_ref[0]).astype(oa_ref.dtype)

    @pl.when(j >= na_tiles)
    def _():
        ob_ref[0] = _bdot(h_scr[...], wb_ref[0])


def _proj_call(x, g, shift, scale, w_in, layer, na, nb_, wab, wg2, bg, tm, tn):
    b, l, d = x.shape
    ta, tb_ = na // tn, nb_ // tn
    ng = wg2.shape[1]
    kern = functools.partial(_proj_kernel, na_tiles=ta)
    return pl.pallas_call(
        kern,
        out_shape=(jax.ShapeDtypeStruct((b, l, na), BF16), jax.ShapeDtypeStruct((b, l, nb_), F32),
                   jax.ShapeDtypeStruct((b, l, ng), F32)),
        grid=(b, l // tm, ta + tb_),
        in_specs=[pl.BlockSpec((1, tm, d), lambda bi, i, j: (bi, i, 0)),
                  pl.BlockSpec((1, d), lambda bi, i, j: (0, 0)),
                  pl.BlockSpec((1, 1, d), lambda bi, i, j: (bi, 0, 0)),
                  pl.BlockSpec((1, 1, d), lambda bi, i, j: (bi, 0, 0)),
                  pl.BlockSpec((1, d, tn), lambda bi, i, j: (layer, 0, jnp.minimum(j, ta - 1))),
                  pl.BlockSpec((1, d, tn), lambda bi, i, j: (layer, 0, jnp.maximum(j, ta))),
                  pl.BlockSpec(wab.shape, lambda bi, i, j: (0, 0)),
                  pl.BlockSpec(wg2.shape, lambda bi, i, j: (0, 0)),
                  pl.BlockSpec((1, ng), lambda bi, i, j: (0, 0))],
        out_specs=(pl.BlockSpec((1, tm, tn), lambda bi, i, j: (bi, i, jnp.minimum(j, ta - 1))),
                   pl.BlockSpec((1, tm, tn), lambda bi, i, j: (bi, i, jnp.maximum(j - ta, 0))),
                   pl.BlockSpec((1, tm, ng), lambda bi, i, j: (bi, i, 0))),
        scratch_shapes=[pltpu.VMEM((tm, d), BF16)],
        compiler_params=_cparams(3),
        name="proj_in",
    )(x, g, shift, scale, w_in, w_in, wab, wg2, bg)


def _na_kernel(rpb_ref, q_ref, k_ref, v_ref, kc_ref, vc_ref, o_ref, tab_ref, *, rows, n_dr, n_dc):
    head = pl.program_id(0)
    scale = HEAD_DIM ** -0.5 * LOG2E
    wr = NA_ROWS

    qc = lax.broadcasted_iota(jnp.int32, (GRID_W, 2 * GRID_W), 0)
    lane = lax.broadcasted_iota(jnp.int32, (GRID_W, 2 * GRID_W), 1)
    kc = lane & (GRID_W - 1)
    dc = jnp.clip(kc - qc + (NA_COLS - 1), 0, n_dc - 1)
    cstart = jnp.clip(qc - NA_COLS // 2, 0, GRID_W - NA_COLS)
    in_win = (kc >= cstart) & (kc < cstart + NA_COLS)
    left = lane < GRID_W
    base_off = head * (n_dr * n_dc)

    @pl.when(pl.program_id(1) == 0)
    def _():
        for d in range(-1, n_dr):
            acc = jnp.zeros((GRID_W, 2 * GRID_W), F32)
            for c in range(n_dc):
                if d < 0:
                    val = rpb_ref[base_off + (d + 1) * n_dc + c]
                elif d + 1 >= n_dr:
                    val = rpb_ref[base_off + d * n_dc + c]
                else:
                    val = jnp.where(left, rpb_ref[base_off + d * n_dc + c], rpb_ref[base_off + (d + 1) * n_dc + c])
                acc = jnp.where(dc == c, val, acc)
            keep = in_win & ~left if d < 0 else (in_win & left if d + 1 >= n_dr else in_win)
            tab_ref[d + 1] = jnp.where(keep, acc * LOG2E, -jnp.inf)

    kctx = kc_ref[0]
    vctx = vc_ref[0]
    grp = NA_GROUP
    union = grp + wr
    neg_inf = jnp.full((GRID_W, 2 * GRID_W), -jnp.inf, F32)

    def scores(gi):
        r0 = gi * grp
        ustart = jnp.clip(r0 - wr // 2, 0, rows - union)
        q_rows = pl.ds(pl.multiple_of(r0 * GRID_W, grp * GRID_W), grp * GRID_W)
        kv_rows = pl.ds(pl.multiple_of(ustart * GRID_W, GRID_W), union * GRID_W)
        q = q_ref[0, q_rows, :]
        kb = k_ref[0, kv_rows, :]
        biases = []
        for g in range(grp):
            rq = r0 + g
            rstart = jnp.clip(rq - wr // 2, 0, rows - wr)
            tiles = []
            for j in range(union // 2):
                ku = ustart + 2 * j
                d = ku - rq + (NA_ROWS - 1)
                ok_l = ((ku >= rstart) & (ku < rstart + wr)).astype(jnp.int32)
                ok_r = ((ku + 1 >= rstart) & (ku + 1 < rstart + wr)).astype(jnp.int32)
                tile = tab_ref[jnp.clip(d, -1, n_dr - 1) + 1]
                tiles.append(jnp.where(jnp.where(left, ok_l, ok_r) > 0, tile, neg_inf))
            biases.append(jnp.concatenate(tiles, axis=1))
        s_loc = _bdot_nt(q, kb) * scale + jnp.concatenate(biases, axis=0)
        s_ctx = _bdot_nt(q, kctx) * scale
        return s_loc, s_ctx, q_rows, kv_rows

    def softmax(s_loc, s_ctx):
        m = jnp.maximum(jnp.max(s_loc, axis=-1, keepdims=True), jnp.max(s_ctx, axis=-1, keepdims=True))
        p_loc = jnp.exp2(s_loc - m)
        p_ctx = jnp.exp2(s_ctx - m)
        denom = jnp.sum(p_loc, axis=-1, keepdims=True) + jnp.sum(p_ctx, axis=-1, keepdims=True)
        return p_loc.astype(BF16), p_ctx.astype(BF16), denom

    def values(p_loc, p_ctx, denom, q_rows, kv_rows):
        o = _bdot(p_loc, v_ref[0, kv_rows, :]) + _bdot(p_ctx, vctx)
        o_ref[0, q_rows, :] = (o / denom).astype(o_ref.dtype)

    def group_pair(t, carry):
        sa = scores(2 * t)
        sb = scores(2 * t + 1)
        pa = softmax(sa[0], sa[1])
        pb = softmax(sb[0], sb[1])
        values(*pa, sa[2], sa[3])
        values(*pb, sb[2], sb[3])
        return carry

    lax.fori_loop(0, rows // (2 * grp), group_pair, 0)


def _na_call(pa, pa_ctx, rpb):
    b, l, _ = pa.shape
    lc = pa_ctx.shape[1]
    nh, n_dr, n_dc = rpb.shape
    rows = l // GRID_W
    kern = functools.partial(_na_kernel, rows=rows, n_dr=n_dr, n_dc=n_dc)
    return pl.pallas_call(
        kern,
        out_shape=jax.ShapeDtypeStruct((b, l, nh * HEAD_DIM), BF16),
        grid=(nh, b),
        in_specs=[pl.BlockSpec(memory_space=pltpu.SMEM),
                  pl.BlockSpec((1, l, HEAD_DIM), lambda h, bi: (bi, 0, h)),
                  pl.BlockSpec((1, l, HEAD_DIM), lambda h, bi: (bi, 0, nh + h)),
                  pl.BlockSpec((1, l, HEAD_DIM), lambda h, bi: (bi, 0, 2 * nh + h)),
                  pl.BlockSpec((1, lc, HEAD_DIM), lambda h, bi: (bi, 0, nh + h)),
                  pl.BlockSpec((1, lc, HEAD_DIM), lambda h, bi: (bi, 0, 2 * nh + h))],
        out_specs=pl.BlockSpec((1, l, HEAD_DIM), lambda h, bi: (bi, 0, h)),
        scratch_shapes=[pltpu.VMEM((n_dr + 1, GRID_W, 2 * GRID_W), F32)],
        compiler_params=_cparams(2),
        name="na_attention",
    )(rpb.reshape(-1), pa, pa, pa, pa_ctx, pa_ctx)


def _ctx_attn_kernel(q_ref, k_ref, v_ref, o_ref):
    q = q_ref[0]
    s = _bdot_nt(q, k_ref[0]) * (HEAD_DIM ** -0.5)
    m = jnp.max(s, axis=-1, keepdims=True)
    p = jnp.exp(s - m)
    denom = jnp.sum(p, axis=-1, keepdims=True)
    o_ref[0] = (_bdot(p.astype(BF16), v_ref[0]) / denom).astype(o_ref.dtype)


def _ctx_attn_call(pa_ctx, nh):
    b, lc, _ = pa_ctx.shape
    return pl.pallas_call(
        _ctx_attn_kernel,
        out_shape=jax.ShapeDtypeStruct((b, lc, nh * HEAD_DIM), BF16),
        grid=(b, nh),
        in_specs=[pl.BlockSpec((1, lc, HEAD_DIM), lambda bi, h: (bi, 0, h)),
                  pl.BlockSpec((1, lc, HEAD_DIM), lambda bi, h: (bi, 0, nh + h)),
                  pl.BlockSpec((1, lc, HEAD_DIM), lambda bi, h: (bi, 0, 2 * nh + h))],
        out_specs=pl.BlockSpec((1, lc, HEAD_DIM), lambda bi, h: (bi, 0, h)),
        compiler_params=_cparams(2),
        name="ctx_attention",
    )(pa_ctx, pa_ctx, pa_ctx)


_GLA_LEVELS = (32, 16, 8, 4, 2, 1)


def _gla_dir_consts(rev):
    c = GLA_CHUNK
    ii = lax.broadcasted_iota(jnp.int32, (c, c), 0)
    jj = lax.broadcasted_iota(jnp.int32, (c, c), 1)
    tri = ((jj >= ii) if rev else (jj <= ii)).astype(BF16)
    i2 = lax.broadcasted_iota(jnp.int32, (c, 2 * c), 0)
    j2 = lax.broadcasted_iota(jnp.int32, (c, 2 * c), 1) & (c - 1)
    masks = [i2 == j2]
    for s in _GLA_LEVELS:
        same = (i2 // (2 * s)) == (j2 // (2 * s))
        qi = i2 % (2 * s)
        kj = j2 % (2 * s)
        masks.append(same & ((qi < s) & (kj >= s) if rev else (qi >= s) & (kj < s)))
    return tri, masks


def _gla_ref_rows(cum, s, rev, sub8):
    c, w = cum.shape
    blk = 2 * s
    off = s - 1 if rev else s
    pieces = []
    if blk >= 8:
        for b0 in range(0, c, blk):
            pieces.append(jnp.broadcast_to(cum[b0 + off:b0 + off + 1, :], (blk, w)))
    else:
        for g0 in range(0, c, 8):
            out = None
            for b0 in range(0, 8, blk):
                cand = jnp.broadcast_to(cum[g0 + b0 + off:g0 + b0 + off + 1, :], (8, w))
                out = cand if out is None else jnp.where(sub8 >= b0, cand, out)
            pieces.append(out)
    return jnp.concatenate(pieces, axis=0)


def _gla_prep(rev, tri, sub8, q, k, g):
    c = GLA_CHUNK
    g1 = g.astype(BF16)
    r1 = g - g1.astype(F32)
    g2 = r1.astype(BF16)
    g3 = (r1 - g2.astype(F32)).astype(BF16)
    cum = (_bdot(tri, g1) + _bdot(tri, g2) + _bdot(tri, g3)) * LOG2E
    end_row = 0 if rev else c - 1
    last = cum[end_row:end_row + 1, :]
    q_in = (q * jnp.exp2(cum)).astype(BF16)
    k_out = (k * jnp.exp2(last - cum)).astype(BF16)
    e_last = jnp.exp2(last)

    qs = [q.astype(BF16)]
    ks = [k.astype(BF16)]
    for s in _GLA_LEVELS:
        dq = cum - _gla_ref_rows(cum, s, rev, sub8)
        e = jnp.exp2(jnp.minimum(dq, -dq))
        qs.append((q * e).astype(BF16))
        ks.append((k * e).astype(BF16))
    return qs, ks, q_in, k_out, e_last


def _gla_scores(masks, left128, qs, ks, n_pairs):
    c = GLA_CHUNK
    attns = [jnp.zeros((c, 2 * c), F32) for _ in range(n_pairs)]
    for lv_i in range(len(masks)):
        for p in range(n_pairs):
            ksl = slice(p * 128, (p + 1) * 128)
            kp = ks[lv_i][:, ksl]
            kbd = jnp.concatenate([jnp.where(left128, kp, 0), jnp.where(left128, 0, kp)], axis=0)
            attns[p] = jnp.where(masks[lv_i], _bdot_nt(qs[lv_i][:, ksl], kbd), attns[p])
    return [a.astype(BF16) for a in attns]


def _gla_finish(left256, st_diag, attns, q_in, k_out, e_last, v, st_ref, n_pairs):
    outs = []
    for p in range(n_pairs):
        ksl = slice(p * 128, (p + 1) * 128)
        vp16 = v[:, p * 256:(p + 1) * 256].astype(BF16)
        vbd = jnp.concatenate([jnp.where(left256, vp16, 0), jnp.where(left256, 0, vp16)], axis=0)
        st = st_ref[p]
        outs.append(_bdot(attns[p], vbd) + _bdot_nt(q_in[:, ksl], st.astype(BF16)))
        upd = lax.dot_general(vp16, k_out[:, ksl], (((0,), (0,)), ((), ())), preferred_element_type=F32)
        st_ref[p] = st * e_last[:, ksl] + jnp.where(st_diag, upd, 0.0)
    return jnp.concatenate(outs, axis=1)


def _gla_kernel(qf_ref, kf_ref, vf_ref, gf_ref, cosf_ref, sinf_ref,
                qr_ref, kr_ref, vr_ref, gr_ref, cosr_ref, sinr_ref, s0f_ref, s0r_ref,
                of_ref, or_ref, sff_ref, sfr_ref, stf_scr, str_scr, *, n_chunks, n_pairs):
    c = GLA_CHUNK
    blk_i = pl.program_id(1)

    @pl.when(blk_i == 0)
    def _():
        stf_scr[...] = s0f_ref[0]
        str_scr[...] = s0r_ref[0]

    l2 = lax.broadcasted_iota(jnp.int32, (c, 2 * c), 1)
    left128 = l2 < c
    lane32 = l2 & 31
    left256 = lax.broadcasted_iota(jnp.int32, (c, 2 * GLA_DV), 1) < GLA_DV
    sr = lax.broadcasted_iota(jnp.int32, (2 * GLA_DV, 2 * GLA_DK), 0)
    sc_ = lax.broadcasted_iota(jnp.int32, (2 * GLA_DV, 2 * GLA_DK), 1)
    st_diag = (sr // GLA_DV) == (sc_ // GLA_DK)
    sub8 = lax.broadcasted_iota(jnp.int32, (8, n_pairs * 128), 0)
    tri_f, masks_f = _gla_dir_consts(False)
    tri_r, masks_r = _gla_dir_consts(True)

    def rope(x, cos, sin):
        outs = []
        for p in range(n_pairs):
            xs = x[:, p * 128:(p + 1) * 128]
            up = pltpu.roll(xs, 16, axis=1)
            dn = pltpu.roll(xs, 112, axis=1)
            sw = jnp.where(lane32 < 16, dn, up)
            outs.append(xs * cos + sw * sin)
        return jnp.concatenate(outs, axis=1)

    def prep(rev, tri, rows, q_ref, k_ref, g_ref, cos_ref, sin_ref):
        cos = cos_ref[rows, :]
        sin = sin_ref[rows, :]
        q = rope(q_ref[0, rows, :], cos, sin) * (GLA_DK ** -0.5)
        k = rope(k_ref[0, rows, :], cos, sin)
        return _gla_prep(rev, tri, sub8, q, k, g_ref[0, rows, :])

    def chunk(ci, carry):
        rows_f = pl.ds(pl.multiple_of(ci * c, c), c)
        rows_r = pl.ds(pl.multiple_of((n_chunks - 1 - ci) * c, c), c)
        qs_f, ks_f, qin_f, kout_f, el_f = prep(False, tri_f, rows_f, qf_ref, kf_ref, gf_ref, cosf_ref, sinf_ref)
        qs_r, ks_r, qin_r, kout_r, el_r = prep(True, tri_r, rows_r, qr_ref, kr_ref, gr_ref, cosr_ref, sinr_ref)
        at_f = _gla_scores(masks_f, left128, qs_f, ks_f, n_pairs)
        at_r = _gla_scores(masks_r, left128, qs_r, ks_r, n_pairs)
        of_ref[0, rows_f, :] = _gla_finish(left256, st_diag, at_f, qin_f, kout_f, el_f, vf_ref[0, rows_f, :],
                                           stf_scr, n_pairs)
        or_ref[0, rows_r, :] = _gla_finish(left256, st_diag, at_r, qin_r, kout_r, el_r, vr_ref[0, rows_r, :],
                                           str_scr, n_pairs)
        return carry

    lax.fori_loop(0, n_chunks, chunk, 0)

    @pl.when(blk_i == pl.num_programs(1) - 1)
    def _():
        sff_ref[0] = stf_scr[...]
        sfr_ref[0] = str_scr[...]


def _gla_call(pb, lg, cos, sin, s0f, s0r, *, tb):
    b, l, _ = pb.shape
    nqk = lg.shape[2] // 2
    n_pairs = nqk // 128
    nv = n_pairs * 2 * GLA_DV
    nb = l // tb
    kern = functools.partial(_gla_kernel, n_chunks=tb // GLA_CHUNK, n_pairs=n_pairs)
    st_shape = (n_pairs, 2 * GLA_DV, 2 * GLA_DK)

    def dir_specs(tok, gate_blk):
        return [pl.BlockSpec((1, tb, nqk), lambda bi, i: (bi, tok(i), 0)),
                pl.BlockSpec((1, tb, nqk), lambda bi, i: (bi, tok(i), 1)),
                pl.BlockSpec((1, tb, nv), lambda bi, i: (bi, tok(i), 1)),
                pl.BlockSpec((1, tb, nqk), lambda bi, i: (bi, tok(i), gate_blk)),
                pl.BlockSpec((tb, 128), lambda bi, i: (tok(i), 0)),
                pl.BlockSpec((tb, 128), lambda bi, i: (tok(i), 0))]

    fwd_tok = lambda i: i
    rev_tok = lambda i: nb - 1 - i
    st_spec = pl.BlockSpec((1,) + st_shape, lambda bi, i: (bi, 0, 0, 0))
    return pl.pallas_call(
        kern,
        out_shape=(jax.ShapeDtypeStruct((b, l, nv), F32), jax.ShapeDtypeStruct((b, l, nv), F32),
                   jax.ShapeDtypeStruct((b,) + st_shape, F32), jax.ShapeDtypeStruct((b,) + st_shape, F32)),
        grid=(b, nb),
        in_specs=dir_specs(fwd_tok, 0) + dir_specs(rev_tok, 1) + [st_spec, st_spec],
        out_specs=(pl.BlockSpec((1, tb, nv), lambda bi, i: (bi, fwd_tok(i), 0)),
                   pl.BlockSpec((1, tb, nv), lambda bi, i: (bi, rev_tok(i), 0)),
                   st_spec, st_spec),
        scratch_shapes=[pltpu.VMEM(st_shape, F32), pltpu.VMEM(st_shape, F32)],
        compiler_params=_cparams(2),
        name="gla_scan",
    )(pb, pb, pb, lg, cos, sin, pb, pb, pb, lg, cos, sin, s0f, s0r)


def _rope_tables(seq_len):
    t = np.arange(seq_len)
    row = (t // GRID_W).astype(np.float32)
    col = (t % GRID_W).astype(np.float32)
    nf = GLA_DK // 4
    inv = np.float32(ROPE_BASE) ** (-np.arange(nf, dtype=np.float32) / np.float32(nf))
    ar = row[:, None] * inv[None, :]
    ac = col[:, None] * inv[None, :]
    cos = np.concatenate([np.cos(ar), np.cos(ar), np.cos(ac), np.cos(ac)], axis=1)
    sin = np.concatenate([-np.sin(ar), np.sin(ar), -np.sin(ac), np.sin(ac)], axis=1)
    return jnp.asarray(np.tile(cos, (1, 2)), F32), jnp.asarray(np.tile(sin, (1, 2)), F32)


def _combine_kernel(oa_ref, of_ref, ob_ref, gb_ref, gg_ref, w_ref, x_ref, g1_ref, o_ref, *, n_heads):
    ob = of_ref[0] + ob_ref[0]
    parts = []
    for h in range(n_heads):
        oh = ob[:, h * GLA_DV:(h + 1) * GLA_DV]
        ms = jnp.mean(oh * oh, axis=-1, keepdims=True)
        parts.append(oh * lax.rsqrt(ms + EPS))
    obn = jnp.concatenate(parts, axis=1) * gg_ref[...]
    gb = gb_ref[0]
    yb = obn * (gb / (1.0 + jnp.exp(-gb)))
    na_w = oa_ref.shape[2]
    y = _bdot(oa_ref[0], w_ref[0, 0:na_w, :]) + _bdot(yb.astype(BF16), w_ref[0, na_w:, :])
    o_ref[0] = x_ref[0] + g1_ref[0] * y


def _combine_call(oa, of, ob, pb, gla_g, w_out, layer, x, g1, tm):
    b, l, d = x.shape
    na_w = oa.shape[2]
    gl_w = of.shape[2]
    n_heads = gl_w // GLA_DV
    gate_blk = (pb.shape[2] - gl_w) // gl_w
    kern = functools.partial(_combine_kernel, n_heads=n_heads)
    return pl.pallas_call(
        kern,
        out_shape=jax.ShapeDtypeStruct((b, l, d), F32),
        grid=(b, l // tm),
        in_specs=[pl.BlockSpec((1, tm, na_w), lambda bi, i: (bi, i, 0)),
                  pl.BlockSpec((1, tm, gl_w), lambda bi, i: (bi, i, 0)),
                  pl.BlockSpec((1, tm, gl_w), lambda bi, i: (bi, i, 0)),
                  pl.BlockSpec((1, tm, gl_w), lambda bi, i: (bi, i, gate_blk)),
                  pl.BlockSpec((1, gl_w), lambda bi, i: (0, 0)),
                  pl.BlockSpec((1,) + w_out.shape[1:], lambda bi, i: (layer, 0, 0)),
                  pl.BlockSpec((1, tm, d), lambda bi, i: (bi, i, 0)),
                  pl.BlockSpec((1, 1, d), lambda bi, i: (bi, 0, 0))],
        out_specs=pl.BlockSpec((1, tm, d), lambda bi, i: (bi, i, 0)),
        compiler_params=_cparams(2),
        name="combine_out",
    )(oa, of, ob, pb, gla_g, w_out, x, g1)


def _pool_kernel(x_ref, xp_ref, xn_ref, g_ref, sh_ref, sc_ref, w_ref, ps_ref, g1_ref, o_ref, h_scr,
                 *, tm, seq_len):
    i = pl.program_id(1)
    n_i = pl.num_programs(1)
    g = g_ref[...]
    sh = sh_ref[0]
    sc = sc_ref[0]
    h_scr[HALO:HALO + tm, :] = _norm_mod(x_ref[0], g, sh, sc)
    h_scr[0:HALO, :] = jnp.where(i > 0, _norm_mod(xp_ref[0], g, sh, sc), 0.0)
    h_scr[HALO + tm:, :] = jnp.where(i < n_i - 1, _norm_mod(xn_ref[0], g, sh, sc), 0.0)

    t = i * tm + lax.broadcasted_iota(jnp.int32, (tm, 1), 0)
    n_ext = tm + 2 * HALO
    grp = h_scr.shape[1] // len(POOL_WINDOWS)
    ys = []
    for gi, w in enumerate(POOL_WINDOWS):
        parts = []
        for c0 in range(gi * grp, (gi + 1) * grp, 128):
            e = h_scr[:, c0:c0 + 128]
            acc = e + pltpu.roll(e, 1, axis=0)
            n = 2
            while n < w:
                acc = pltpu.roll(acc, n // 2, axis=0) + pltpu.roll(acc, n_ext - n // 2, axis=0)
                n *= 2
            parts.append(acc[HALO:HALO + tm])
        cols = slice(gi * grp, (gi + 1) * grp)
        cnt = (jnp.minimum(t + w // 2, seq_len) - jnp.maximum(t - w // 2, 0)).astype(F32)
        pooled = jnp.concatenate(parts, axis=1) / cnt - h_scr[HALO:HALO + tm, cols]
        ys.append(_bdot(pooled.astype(BF16), w_ref[0, gi]))
    y = jnp.concatenate(ys, axis=1) * ps_ref[...]
    o_ref[0] = x_ref[0] + g1_ref[0] * y


def _halo_specs(tm, l, d):
    per = tm // HALO
    last = l // HALO - 1
    prev = pl.BlockSpec((1, HALO, d), lambda bi, i, *_: (bi, jnp.maximum(i * per - 1, 0), 0))
    nxt = pl.BlockSpec((1, HALO, d), lambda bi, i, *_: (bi, jnp.minimum((i + 1) * per, last), 0))
    return prev, nxt


def _pool_call(x, g, shift, scale, pool_w, layer, pool_scale, g1, tm):
    b, l, d = x.shape
    prev, nxt = _halo_specs(tm, l, d)
    kern = functools.partial(_pool_kernel, tm=tm, seq_len=l)
    return pl.pallas_call(
        kern,
        out_shape=jax.ShapeDtypeStruct((b, l, d), F32),
        grid=(b, l // tm),
        in_specs=[pl.BlockSpec((1, tm, d), lambda bi, i: (bi, i, 0)), prev, nxt,
                  pl.BlockSpec((1, d), lambda bi, i: (0, 0)),
                  pl.BlockSpec((1, 1, d), lambda bi, i: (bi, 0, 0)),
                  pl.BlockSpec((1, 1, d), lambda bi, i: (bi, 0, 0)),
                  pl.BlockSpec((1,) + pool_w.shape[1:], lambda bi, i: (layer, 0, 0, 0)),
                  pl.BlockSpec((1, d), lambda bi, i: (0, 0)),
                  pl.BlockSpec((1, 1, d), lambda bi, i: (bi, 0, 0))],
        out_specs=pl.BlockSpec((1, tm, d), lambda bi, i: (bi, i, 0)),
        scratch_shapes=[pltpu.VMEM((tm + 2 * HALO, d), F32)],
        compiler_params=_cparams(2),
        name="pool_mixer",
    )(x, x, x, g, shift, scale, pool_w, pool_scale, g1)


def _ffn_kernel(x_ref, xp_ref, xn_ref, g_ref, sh_ref, sc_ref, wv_ref, wg_ref, cw_ref, cb_ref, wd_ref,
                g2_ref, fg_ref, o_ref, h_scr, acc_scr, act_scr, *, tm, seg, final_norm):
    i = pl.program_id(1)
    j = pl.program_id(2)
    n_f = pl.num_programs(2) - 1

    def prologue():
        g = g_ref[...]
        sh = sh_ref[0]
        sc = sc_ref[0]
        h_scr[HALO:HALO + tm, :] = _norm_mod(x_ref[0], g, sh, sc).astype(BF16)
        hp = _norm_mod(xp_ref[0], g, sh, sc)
        hn = _norm_mod(xn_ref[0], g, sh, sc)
        if seg % tm == 0:
            hp = jnp.where(lax.rem(i * tm, seg) == 0, 0.0, hp)
            hn = jnp.where(lax.rem((i + 1) * tm, seg) == 0, 0.0, hn)
        h_scr[0:HALO, :] = hp.astype(BF16)
        h_scr[HALO + tm:, :] = hn.astype(BF16)
        acc_scr[...] = jnp.zeros_like(acc_scr)

    rows_all = tm + 2 * HALO
    tf = wg_ref.shape[2]

    def up_matmuls():
        ug = _bdot(h_scr[...], wg_ref[0])
        uv = _bdot(h_scr[HALO:HALO + tm, :], wv_ref[0])
        return ug, uv

    def activation(ug, uv):
        f_cols = pl.ds(pl.multiple_of(j * tf, tf), tf)
        cw = cw_ref[0, :, f_cols]
        g_prev = pltpu.roll(ug, 1, axis=0)[HALO:HALO + tm]
        g_next = pltpu.roll(ug, rows_all - 1, axis=0)[HALO:HALO + tm]
        if seg % tm != 0:
            pos = lax.rem(i * tm + lax.broadcasted_iota(jnp.int32, (tm, 1), 0), seg)
            g_prev = jnp.where(pos == 0, 0.0, g_prev)
            g_next = jnp.where(pos == seg - 1, 0.0, g_next)
        gate = g_prev * cw[0:1] + ug[HALO:HALO + tm] * cw[1:2] + g_next * cw[2:3] + cb_ref[0, :, f_cols]
        act = 0.5 * gate * (1.0 + lax.erf(gate * (2.0 ** -0.5))) * uv
        act_scr[lax.rem(j, 2)] = act.astype(BF16)

    def down():
        acc_scr[...] += _bdot(act_scr[lax.rem(j + 1, 2)], wd_ref[0].astype(BF16))

    @pl.when(j == 0)
    def _():
        prologue()
        activation(*up_matmuls())

    @pl.when((j > 0) & (j < n_f))
    def _():
        ug, uv = up_matmuls()
        down()
        activation(ug, uv)

    @pl.when(j == n_f)
    def _():
        down()
        y = x_ref[0] + g2_ref[0] * acc_scr[...]
        if final_norm:
            ms = jnp.mean(y * y, axis=-1, keepdims=True)
            y = y * lax.rsqrt(ms + EPS) * fg_ref[...]
        o_ref[0] = y


def _ffn_call(x, g, shift, scale, w_up, conv_w, conv_b, w_down, layer, g2, final_g, tm, tf, seg, final_norm):
    b, l, d = x.shape
    nf = w_down.shape[1] // tf
    prev, nxt = _halo_specs(tm, l, d)
    kern = functools.partial(_ffn_kernel, tm=tm, seg=seg, final_norm=final_norm)
    return pl.pallas_call(
        kern,
        out_shape=jax.ShapeDtypeStruct((b, l, d), F32),
        grid=(b, l // tm, nf + 1),
        in_specs=[pl.BlockSpec((1, tm, d), lambda bi, i, j: (bi, i, 0)), prev, nxt,
                  pl.BlockSpec((1, d), lambda bi, i, j: (0, 0)),
                  pl.BlockSpec((1, 1, d), lambda bi, i, j: (bi, 0, 0)),
                  pl.BlockSpec((1, 1, d), lambda bi, i, j: (bi, 0, 0)),
                  pl.BlockSpec((1, d, tf), lambda bi, i, j: (layer, 0, jnp.minimum(j, nf - 1))),
                  pl.BlockSpec((1, d, tf), lambda bi, i, j: (layer, 0, nf + jnp.minimum(j, nf - 1))),
                  pl.BlockSpec((1,) + conv_w.shape[1:], lambda bi, i, j: (layer, 0, 0)),
                  pl.BlockSpec((1,) + conv_b.shape[1:], lambda bi, i, j: (layer, 0, 0)),
                  pl.BlockSpec((1, tf, d), lambda bi, i, j: (layer, jnp.maximum(j - 1, 0), 0)),
                  pl.BlockSpec((1, 1, d), lambda bi, i, j: (bi, 0, 0)),
                  pl.BlockSpec((1, d), lambda bi, i, j: (0, 0))],
        out_specs=pl.BlockSpec((1, tm, d), lambda bi, i, j: (bi, i, 0)),
        scratch_shapes=[pltpu.VMEM((tm + 2 * HALO, d), BF16), pltpu.VMEM((tm, d), F32),
                        pltpu.VMEM((2, tm, tf), BF16)],
        compiler_params=_cparams(3),
        name="conv_ffn",
    )(x, x, x, g, shift, scale, w_up, w_up, conv_w, conv_b, w_down, g2, final_g)


def _tile(l, want):
    return min(l, want)


def _even_mixer(x_lat, x_ctx, mods_lat, mods_ctx, n1, e, w_in, w_in_bf, w_gate2, b_gate, rpb, gla_g, w_out_bf,
                need_ctx, rope_lat, rope_ctx):
    sh, sc, g1 = mods_lat
    shc, scc, gc1 = mods_ctx
    nh = rpb.shape[0]
    a_w = 3 * nh * HEAD_DIM
    n_gla = w_gate2.shape[2] // GLA_DK
    b_w = 2 * n_gla * GLA_DK + 2 * n_gla * GLA_DV
    w_ab = jnp.pad(w_in[:, a_w + b_w:], ((0, 0), (0, 128 - 2 * GLA_RANK))).astype(BF16)
    nqk = n_gla * GLA_DK
    wg2 = jnp.zeros((128, 2 * nqk), F32)
    wg2 = wg2.at[0:GLA_RANK, 0:nqk].set(w_gate2[0]).at[GLA_RANK:2 * GLA_RANK, nqk:].set(w_gate2[1]).astype(BF16)
    bg = b_gate.reshape(1, 2 * nqk)

    l = x_lat.shape[1]
    lc = x_ctx.shape[1]
    bsz, d = x_lat.shape[0], x_lat.shape[2]
    pa, pb, lg = _proj_call(x_lat, n1, sh, sc, w_in_bf, e, a_w, b_w, w_ab, wg2, bg, _tile(l, 1024), 512)
    ctx_flat = x_ctx.reshape(1, bsz * lc, d)
    pa_c, pb_c, lg_c = [a.reshape(bsz, lc, -1) for a in
                        _proj_call(ctx_flat, n1, shc[:1], scc[:1], w_in_bf, e, a_w, b_w, w_ab, wg2, bg,
                                   _tile(bsz * lc, 1024), 512)]

    oa = _na_call(pa, pa_c, rpb)

    s0 = jnp.zeros((bsz, nqk // 128, 2 * GLA_DV, 2 * GLA_DK), F32)
    of_c, ob_c, s_f, s_b = _gla_call(pb_c, lg_c, rope_ctx[0], rope_ctx[1], s0, s0, tb=_tile(lc, 512))
    of, ob, _, _ = _gla_call(pb, lg, rope_lat[0], rope_lat[1], s_f, s_b, tb=_tile(l, 512))

    gg = gla_g.reshape(1, -1)
    x_lat = _combine_call(oa, of, ob, pb, gg, w_out_bf, e, x_lat, g1, _tile(l, 512))
    if need_ctx:
        oa_c = _ctx_attn_call(pa_c, nh)
        x_ctx = _combine_call(oa_c, of_c, ob_c, pb_c, gg, w_out_bf, e, x_ctx, gc1, _tile(lc, 512))
    return x_lat, x_ctx


def kernel(x, c, ctx, c_ctx, w_mod, b_mod, norm1_g, norm2_g, w_in, w_gate2, b_gate, rpb, gla_norm_g, w_out,
           pool_w, pool_scale, w_up, conv_w, conv_b, w_down, final_g):
    bsz, seq, d = x.shape
    lc = ctx.shape[1]
    depth = w_mod.shape[0]

    cvec = jnp.zeros((8, d), F32).at[0:bsz].set(c).at[bsz].set(c_ctx)
    mods = _mod_call(cvec, w_mod, b_mod)

    rope_lat = _rope_tables(seq)
    rope_ctx = (jnp.ones((lc, 128), F32), jnp.zeros((lc, 128), F32))

    fg = final_g.reshape(1, d)
    w_in_bf = w_in.astype(BF16)
    w_out_bf = w_out.astype(BF16)
    w_up_bf = w_up.astype(BF16)
    pool_w_bf = pool_w.astype(BF16)
    conv_b3 = conv_b.reshape(depth, 1, -1)
    x_lat, x_ctx = x, ctx
    for i in range(depth):
        is_even = i % 2 == 0
        need_ctx = any(j % 2 == 0 for j in range(i + 1, depth))
        m = mods[i].reshape(8, 6, d)
        lat = [m[0:bsz, k][:, None, :] for k in range(6)]
        cx = [jnp.broadcast_to(m[bsz, k][None, None, :], (bsz, 1, d)) for k in range(6)]
        n1 = norm1_g[i].reshape(1, d)
        n2 = norm2_g[i].reshape(1, d)
        if is_even:
            e = i // 2
            x_lat, x_ctx = _even_mixer(x_lat, x_ctx, lat[0:3], cx[0:3], n1, e, w_in[e], w_in_bf, w_gate2[e], b_gate[e],
                                       rpb[e], gla_norm_g[e], w_out_bf, need_ctx, rope_lat, rope_ctx)
        else:
            o = i // 2
            ps = pool_scale[o].reshape(1, d)
            x_lat = _pool_call(x_lat, n1, lat[0], lat[1], pool_w_bf, o, ps, lat[2], _tile(seq, 512))
            if need_ctx:
                x_ctx = _pool_call(x_ctx, n1, cx[0], cx[1], pool_w_bf, o, ps, cx[2], _tile(lc, 512))
        x_lat = _ffn_call(x_lat, n2, lat[3], lat[4], w_up_bf, conv_w, conv_b3, w_down, i, lat[5], fg,
                          _tile(seq, 512), 512, seg=seq, final_norm=(i == depth - 1))
        if need_ctx:
            ctx_flat = x_ctx.reshape(1, bsz * lc, d)
            x_ctx = _ffn_call(ctx_flat, n2, cx[3][:1], cx[4][:1], w_up_bf, conv_w, conv_b3, w_down, i, cx[5][:1], fg,
                              _tile(bsz * lc, 512), 512, seg=lc, final_norm=False).reshape(bsz, lc, d)
    return x_lat
```

```python
import functools

import numpy as np
import jax
import jax.numpy as jnp
from jax import lax
from jax.experimental import pallas as pl
from jax.experimental.pallas import tpu as pltpu

F32 = jnp.float32
BF16 = jnp.bfloat16

GRID_W = 64
HEAD_DIM = 128
NA_ROWS = 8
NA_COLS = 16
NA_GROUP = 4
GLA_DK = 64
GLA_DV = 128
GLA_RANK = 16
GLA_TAU = 16.0
GLA_CHUNK = 64
POOL_WINDOWS = (2, 4, 8, 16)
ROPE_BASE = 10000.0
EPS = 1e-6
LOG2E = 1.4426950408889634
HALO = 16
VMEM_LIMIT = 56 * 1024 * 1024


def _cparams(n_axes):
    return pltpu.CompilerParams(dimension_semantics=("arbitrary",) * n_axes,
                                vmem_limit_bytes=VMEM_LIMIT)


def _bdot(a, b):
    return jnp.dot(a, b, preferred_element_type=F32)


def _bdot_nt(a, b):
    return lax.dot_general(a, b, (((1,), (1,)), ((), ())), preferred_element_type=F32)


def _norm_mod(x, g, shift, scale):
    ms = jnp.mean(x * x, axis=-1, keepdims=True)
    return x * lax.rsqrt(ms + EPS) * (g * (1.0 + scale)) + shift


def _mod_kernel(c_ref, w_ref, b_ref, o_ref):
    c = c_ref[...]
    s = c / (1.0 + jnp.exp(-c))
    o_ref[0] = _bdot(s.astype(BF16), w_ref[0].astype(BF16)) + b_ref[0]


def _mod_call(cvec, w_mod, b_mod):
    depth, d, n = w_mod.shape
    tn = 1024
    return pl.pallas_call(
        _mod_kernel,
        out_shape=jax.ShapeDtypeStruct((depth, 8, n), F32),
        grid=(depth, n // tn),
        in_specs=[pl.BlockSpec((8, d), lambda l, j: (0, 0)),
                  pl.BlockSpec((1, d, tn), lambda l, j: (l, 0, j)),
                  pl.BlockSpec((1, 1, tn), lambda l, j: (l, 0, j))],
        out_specs=pl.BlockSpec((1, 8, tn), lambda l, j: (l, 0, j)),
        compiler_params=_cparams(2),
        name="mod_matvec",
    )(cvec, w_mod, b_mod.reshape(depth, 1, n))


def _proj_kernel(x_ref, g_ref, sh_ref, sc_ref, wa_ref, wb_ref, wab_ref, wg2_ref, bg_ref,
                 oa_ref, ob_ref, lg_ref, h_scr, *, na_tiles):
    j = pl.program_id(2)

    @pl.when(j == 0)
    def _():
        h = _norm_mod(x_ref[0], g_ref[...], sh_ref[0], sc_ref[0]).astype(BF16)
        h_scr[...] = h
        ab = _bdot(h, wab_ref[...])
        z = _bdot(ab.astype(BF16), wg2_ref[...]) + bg_ref[...]
        log_sig = jnp.minimum(z, 0.0) - jnp.log1p(jnp.exp(-jnp.abs(z)))
        lg_ref[0] = log_sig * (1.0 / GLA_TAU)

    @pl.when(j < na_tiles)
    def _():
        oa_ref[0] = _bdot(h_scr[...], wa_ref[0]).astype(oa_ref.dtype)

    @pl.when(j >= na_tiles)
    def _():
        ob_ref[0] = _bdot(h_scr[...], wb_ref[0])


def _proj_call(x, g, shift, scale, w_in, layer, na, nb_, wab, wg2, bg, tm, tn):
    b, l, d = x.shape
    ta, tb_ = na // tn, nb_ // tn
    ng = wg2.shape[1]
    kern = functools.partial(_proj_kernel, na_tiles=ta)
    return pl.pallas_call(
        kern,
        out_shape=(jax.ShapeDtypeStruct((b, l, na), BF16), jax.ShapeDtypeStruct((b, l, nb_), F32),
                   jax.ShapeDtypeStruct((b, l, ng), F32)),
        grid=(b, l // tm, ta + tb_),
        in_specs=[pl.BlockSpec((1, tm, d), lambda bi, i, j: (bi, i, 0)),
                  pl.BlockSpec((1, d), lambda bi, i, j: (0, 0)),
                  pl.BlockSpec((1, 1, d), lambda bi, i, j: (bi, 0, 0)),
                  pl.BlockSpec((1, 1, d), lambda bi, i, j: (bi, 0, 0)),
                  pl.BlockSpec((1, d, tn), lambda bi, i, j: (layer, 0, jnp.minimum(j, ta - 1))),
                  pl.BlockSpec((1, d, tn), lambda bi, i, j: (layer, 0, jnp.maximum(j, ta))),
                  pl.BlockSpec(wab.shape, lambda bi, i, j: (0, 0)),
                  pl.BlockSpec(wg2.shape, lambda bi, i, j: (0, 0)),
                  pl.BlockSpec((1, ng), lambda bi, i, j: (0, 0))],
        out_specs=(pl.BlockSpec((1, tm, tn), lambda bi, i, j: (bi, i, jnp.minimum(j, ta - 1))),
                   pl.BlockSpec((1, tm, tn), lambda bi, i, j: (bi, i, jnp.maximum(j - ta, 0))),
                   pl.BlockSpec((1, tm, ng), lambda bi, i, j: (bi, i, 0))),
        scratch_shapes=[pltpu.VMEM((tm, d), BF16)],
        compiler_params=_cparams(3),
        name="proj_in",
    )(x, g, shift, scale, w_in, w_in, wab, wg2, bg)


def _na_kernel(rpb_ref, q_ref, k_ref, v_ref, kc_ref, vc_ref, o_ref, tab_ref, *, rows, n_dr, n_dc):
    head = pl.program_id(0)
    scale = HEAD_DIM ** -0.5 * LOG2E
    wr = NA_ROWS

    qc = lax.broadcasted_iota(jnp.int32, (GRID_W, 2 * GRID_W), 0)
    lane = lax.broadcasted_iota(jnp.int32, (GRID_W, 2 * GRID_W), 1)
    kc = lane & (GRID_W - 1)
    dc = jnp.clip(kc - qc + (NA_COLS - 1), 0, n_dc - 1)
    cstart = jnp.clip(qc - NA_COLS // 2, 0, GRID_W - NA_COLS)
    in_win = (kc >= cstart) & (kc < cstart + NA_COLS)
    left = lane < GRID_W
    base_off = head * (n_dr * n_dc)

    @pl.when(pl.program_id(1) == 0)
    def _():
        for d in range(-1, n_dr):
            acc = jnp.zeros((GRID_W, 2 * GRID_W), F32)
            for c in range(n_dc):
                if d < 0:
                    val = rpb_ref[base_off + (d + 1) * n_dc + c]
                elif d + 1 >= n_dr:
                    val = rpb_ref[base_off + d * n_dc + c]
                else:
                    val = jnp.where(left, rpb_ref[base_off + d * n_dc + c], rpb_ref[base_off + (d + 1) * n_dc + c])
                acc = jnp.where(dc == c, val, acc)
            keep = in_win & ~left if d < 0 else (in_win & left if d + 1 >= n_dr else in_win)
            tab_ref[d + 1] = jnp.where(keep, acc * LOG2E, -jnp.inf)

    kctx = kc_ref[0]
    vctx = vc_ref[0]
    grp = NA_GROUP
    union = grp + wr
    neg_inf = jnp.full((GRID_W, 2 * GRID_W), -jnp.inf, F32)

    def scores(gi):
        r0 = gi * grp
        ustart = jnp.clip(r0 - wr // 2, 0, rows - union)
        q_rows = pl.ds(pl.multiple_of(r0 * GRID_W, grp * GRID_W), grp * GRID_W)
        kv_rows = pl.ds(pl.multiple_of(ustart * GRID_W, GRID_W), union * GRID_W)
        q = q_ref[0, q_rows, :]
        kb = k_ref[0, kv_rows, :]
        biases = []
        for g in range(grp):
            rq = r0 + g
            rstart = jnp.clip(rq - wr // 2, 0, rows - wr)
            tiles = []
            for j in range(union // 2):
                ku = ustart + 2 * j
                d = ku - rq + (NA_ROWS - 1)
                ok_l = ((ku >= rstart) & (ku < rstart + wr)).astype(jnp.int32)
                ok_r = ((ku + 1 >= rstart) & (ku + 1 < rstart + wr)).astype(jnp.int32)
                tile = tab_ref[jnp.clip(d, -1, n_dr - 1) + 1]
                tiles.append(jnp.where(jnp.where(left, ok_l, ok_r) > 0, tile, neg_inf))
            biases.append(jnp.concatenate(tiles, axis=1))
        s_loc = _bdot_nt(q, kb) * scale + jnp.concatenate(biases, axis=0)
        s_ctx = _bdot_nt(q, kctx) * scale
        return s_loc, s_ctx, q_rows, kv_rows

    def softmax(s_loc, s_ctx):
        m = jnp.maximum(jnp.max(s_loc, axis=-1, keepdims=True), jnp.max(s_ctx, axis=-1, keepdims=True))
        p_loc = jnp.exp2(s_loc - m)
        p_ctx = jnp.exp2(s_ctx - m)
        denom = jnp.sum(p_loc, axis=-1, keepdims=True) + jnp.sum(p_ctx, axis=-1, keepdims=True)
        return p_loc.astype(BF16), p_ctx.astype(BF16), denom

    def values(p_loc, p_ctx, denom, q_rows, kv_rows):
        o = _bdot(p_loc, v_ref[0, kv_rows, :]) + _bdot(p_ctx, vctx)
        o_ref[0, q_rows, :] = (o / denom).astype(o_ref.dtype)

    def group_pair(t, carry):
        sa = scores(2 * t)
        sb = scores(2 * t + 1)
        pa = softmax(sa[0], sa[1])
        pb = softmax(sb[0], sb[1])
        values(*pa, sa[2], sa[3])
        values(*pb, sb[2], sb[3])
        return carry

    lax.fori_loop(0, rows // (2 * grp), group_pair, 0)


def _na_call(pa, pa_ctx, rpb):
    b, l, _ = pa.shape
    lc = pa_ctx.shape[1]
    nh, n_dr, n_dc = rpb.shape
    rows = l // GRID_W
    kern = functools.partial(_na_kernel, rows=rows, n_dr=n_dr, n_dc=n_dc)
    return pl.pallas_call(
        kern,
        out_shape=jax.ShapeDtypeStruct((b, l, nh * HEAD_DIM), BF16),
        grid=(nh, b),
        in_specs=[pl.BlockSpec(memory_space=pltpu.SMEM),
                  pl.BlockSpec((1, l, HEAD_DIM), lambda h, bi: (bi, 0, h)),
                  pl.BlockSpec((1, l, HEAD_DIM), lambda h, bi: (bi, 0, nh + h)),
                  pl.BlockSpec((1, l, HEAD_DIM), lambda h, bi: (bi, 0, 2 * nh + h)),
                  pl.BlockSpec((1, lc, HEAD_DIM), lambda h, bi: (bi, 0, nh + h)),
                  pl.BlockSpec((1, lc, HEAD_DIM), lambda h, bi: (bi, 0, 2 * nh + h))],
        out_specs=pl.BlockSpec((1, l, HEAD_DIM), lambda h, bi: (bi, 0, h)),
        scratch_shapes=[pltpu.VMEM((n_dr + 1, GRID_W, 2 * GRID_W), F32)],
        compiler_params=_cparams(2),
        name="na_attention",
    )(rpb.reshape(-1), pa, pa, pa, pa_ctx, pa_ctx)


def _ctx_attn_kernel(q_ref, k_ref, v_ref, o_ref):
    q = q_ref[0]
    s = _bdot_nt(q, k_ref[0]) * (HEAD_DIM ** -0.5)
    m = jnp.max(s, axis=-1, keepdims=True)
    p = jnp.exp(s - m)
    denom = jnp.sum(p, axis=-1, keepdims=True)
    o_ref[0] = (_bdot(p.astype(BF16), v_ref[0]) / denom).astype(o_ref.dtype)


def _ctx_attn_call(pa_ctx, nh):
    b, lc, _ = pa_ctx.shape
    return pl.pallas_call(
        _ctx_attn_kernel,
        out_shape=jax.ShapeDtypeStruct((b, lc, nh * HEAD_DIM), BF16),
        grid=(b, nh),
        in_specs=[pl.BlockSpec((1, lc, HEAD_DIM), lambda bi, h: (bi, 0, h)),
                  pl.BlockSpec((1, lc, HEAD_DIM), lambda bi, h: (bi, 0, nh + h)),
                  pl.BlockSpec((1, lc, HEAD_DIM), lambda bi, h: (bi, 0, 2 * nh + h))],
        out_specs=pl.BlockSpec((1, lc, HEAD_DIM), lambda bi, h: (bi, 0, h)),
        compiler_params=_cparams(2),
        name="ctx_attention",
    )(pa_ctx, pa_ctx, pa_ctx)


_GLA_LEVELS = (32, 16, 8, 4, 2, 1)


def _gla_dir_consts(rev):
    c = GLA_CHUNK
    ii = lax.broadcasted_iota(jnp.int32, (c, c), 0)
    jj = lax.broadcasted_iota(jnp.int32, (c, c), 1)
    tri = ((jj >= ii) if rev else (jj <= ii)).astype(BF16)
    i2 = lax.broadcasted_iota(jnp.int32, (c, 2 * c), 0)
    j2 = lax.broadcasted_iota(jnp.int32, (c, 2 * c), 1) & (c - 1)
    masks = [i2 == j2]
    for s in _GLA_LEVELS:
        same = (i2 // (2 * s)) == (j2 // (2 * s))
        qi = i2 % (2 * s)
        kj = j2 % (2 * s)
        masks.append(same & ((qi < s) & (kj >= s) if rev else (qi >= s) & (kj < s)))
    return tri, masks


def _gla_ref_rows(cum, s, rev, sub8):
    c, w = cum.shape
    blk = 2 * s
    off = s - 1 if rev else s
    pieces = []
    if blk >= 8:
        for b0 in range(0, c, blk):
            pieces.append(jnp.broadcast_to(cum[b0 + off:b0 + off + 1, :], (blk, w)))
    else:
        for g0 in range(0, c, 8):
            out = None
            for b0 in range(0, 8, blk):
                cand = jnp.broadcast_to(cum[g0 + b0 + off:g0 + b0 + off + 1, :], (8, w))
                out = cand if out is None else jnp.where(sub8 >= b0, cand, out)
            pieces.append(out)
    return jnp.concatenate(pieces, axis=0)


def _gla_prep(rev, tri, sub8, q, k, g):
    c = GLA_CHUNK
    g1 = g.astype(BF16)
    r1 = g - g1.astype(F32)
    g2 = r1.astype(BF16)
    g3 = (r1 - g2.astype(F32)).astype(BF16)
    cum = (_bdot(tri, g1) + _bdot(tri, g2) + _bdot(tri, g3)) * LOG2E
    end_row = 0 if rev else c - 1
    last = cum[end_row:end_row + 1, :]
    q_in = (q * jnp.exp2(cum)).astype(BF16)
    k_out = (k * jnp.exp2(last - cum)).astype(BF16)
    e_last = jnp.exp2(last)

    qs = [q.astype(BF16)]
    ks = [k.astype(BF16)]
    for s in _GLA_LEVELS:
        dq = cum - _gla_ref_rows(cum, s, rev, sub8)
        e = jnp.exp2(jnp.minimum(dq, -dq))
        qs.append((q * e).astype(BF16))
        ks.append((k * e).astype(BF16))
    return qs, ks, q_in, k_out, e_last


def _gla_scores(masks, left128, qs, ks, n_pairs):
    c = GLA_CHUNK
    attns = [jnp.zeros((c, 2 * c), F32) for _ in range(n_pairs)]
    for lv_i in range(len(masks)):
        for p in range(n_pairs):
            ksl = slice(p * 128, (p + 1) * 128)
            kp = ks[lv_i][:, ksl]
            kbd = jnp.concatenate([jnp.where(left128, kp, 0), jnp.where(left128, 0, kp)], axis=0)
            attns[p] = jnp.where(masks[lv_i], _bdot_nt(qs[lv_i][:, ksl], kbd), attns[p])
    return [a.astype(BF16) for a in attns]


def _gla_finish(left256, st_diag, attns, q_in, k_out, e_last, v, st_ref, n_pairs):
    outs = []
    for p in range(n_pairs):
        ksl = slice(p * 128, (p + 1) * 128)
        vp16 = v[:, p * 256:(p + 1) * 256].astype(BF16)
        vbd = jnp.concatenate([jnp.where(left256, vp16, 0), jnp.where(left256, 0, vp16)], axis=0)
        st = st_ref[p]
        outs.append(_bdot(attns[p], vbd) + _bdot_nt(q_in[:, ksl], st.astype(BF16)))
        upd = lax.dot_general(vp16, k_out[:, ksl], (((0,), (0,)), ((), ())), preferred_element_type=F32)
        st_ref[p] = st * e_last[:, ksl] + jnp.where(st_diag, upd, 0.0)
    return jnp.concatenate(outs, axis=1)


def _gla_kernel(qf_ref, kf_ref, vf_ref, gf_ref, cosf_ref, sinf_ref,
                qr_ref, kr_ref, vr_ref, gr_ref, cosr_ref, sinr_ref, s0f_ref, s0r_ref,
                of_ref, or_ref, sff_ref, sfr_ref, stf_scr, str_scr, *, n_chunks, n_pairs):
    c = GLA_CHUNK
    blk_i = pl.program_id(1)

    @pl.when(blk_i == 0)
    def _():
        stf_scr[...] = s0f_ref[0]
        str_scr[...] = s0r_ref[0]

    l2 = lax.broadcasted_iota(jnp.int32, (c, 2 * c), 1)
    left128 = l2 < c
    lane32 = l2 & 31
    left256 = lax.broadcasted_iota(jnp.int32, (c, 2 * GLA_DV), 1) < GLA_DV
    sr = lax.broadcasted_iota(jnp.int32, (2 * GLA_DV, 2 * GLA_DK), 0)
    sc_ = lax.broadcasted_iota(jnp.int32, (2 * GLA_DV, 2 * GLA_DK), 1)
    st_diag = (sr // GLA_DV) == (sc_ // GLA_DK)
    sub8 = lax.broadcasted_iota(jnp.int32, (8, n_pairs * 128), 0)
    tri_f, masks_f = _gla_dir_consts(False)
    tri_r, masks_r = _gla_dir_consts(True)

    def rope(x, cos, sin):
        outs = []
        for p in range(n_pairs):
            xs = x[:, p * 128:(p + 1) * 128]
            up = pltpu.roll(xs, 16, axis=1)
            dn = pltpu.roll(xs, 112, axis=1)
            sw = jnp.where(lane32 < 16, dn, up)
            outs.append(xs * cos + sw * sin)
        return jnp.concatenate(outs, axis=1)

    def prep(rev, tri, rows, q_ref, k_ref, g_ref, cos_ref, sin_ref):
        cos = cos_ref[rows, :]
        sin = sin_ref[rows, :]
        q = rope(q_ref[0, rows, :], cos, sin) * (GLA_DK ** -0.5)
        k = rope(k_ref[0, rows, :], cos, sin)
        return _gla_prep(rev, tri, sub8, q, k, g_ref[0, rows, :])

    def chunk(ci, carry):
        rows_f = pl.ds(pl.multiple_of(ci * c, c), c)
        rows_r = pl.ds(pl.multiple_of((n_chunks - 1 - ci) * c, c), c)
        qs_f, ks_f, qin_f, kout_f, el_f = prep(False, tri_f, rows_f, qf_ref, kf_ref, gf_ref, cosf_ref, sinf_ref)
        qs_r, ks_r, qin_r, kout_r, el_r = prep(True, tri_r, rows_r, qr_ref, kr_ref, gr_ref, cosr_ref, sinr_ref)
        at_f = _gla_scores(masks_f, left128, qs_f, ks_f, n_pairs)
        at_r = _gla_scores(masks_r, left128, qs_r, ks_r, n_pairs)
        of_ref[0, rows_f, :] = _gla_finish(left256, st_diag, at_f, qin_f, kout_f, el_f, vf_ref[0, rows_f, :],
                                           stf_scr, n_pairs)
        or_ref[0, rows_r, :] = _gla_finish(left256, st_diag, at_r, qin_r, kout_r, el_r, vr_ref[0, rows_r, :],
                                           str_scr, n_pairs)
        return carry

    lax.fori_loop(0, n_chunks, chunk, 0)

    @pl.when(blk_i == pl.num_programs(1) - 1)
    def _():
        sff_ref[0] = stf_scr[...]
        sfr_ref[0] = str_scr[...]


def _gla_call(pb, lg, cos, sin, s0f, s0r, *, tb):
    b, l, _ = pb.shape
    nqk = lg.shape[2] // 2
    n_pairs = nqk // 128
    nv = n_pairs * 2 * GLA_DV
    nb = l // tb
    kern = functools.partial(_gla_kernel, n_chunks=tb // GLA_CHUNK, n_pairs=n_pairs)
    st_shape = (n_pairs, 2 * GLA_DV, 2 * GLA_DK)

    def dir_specs(tok, gate_blk):
        return [pl.BlockSpec((1, tb, nqk), lambda bi, i: (bi, tok(i), 0)),
                pl.BlockSpec((1, tb, nqk), lambda bi, i: (bi, tok(i), 1)),
                pl.BlockSpec((1, tb, nv), lambda bi, i: (bi, tok(i), 1)),
                pl.BlockSpec((1, tb, nqk), lambda bi, i: (bi, tok(i), gate_blk)),
                pl.BlockSpec((tb, 128), lambda bi, i: (tok(i), 0)),
                pl.BlockSpec((tb, 128), lambda bi, i: (tok(i), 0))]

    fwd_tok = lambda i: i
    rev_tok = lambda i: nb - 1 - i
    st_spec = pl.BlockSpec((1,) + st_shape, lambda bi, i: (bi, 0, 0, 0))
    return pl.pallas_call(
        kern,
        out_shape=(jax.ShapeDtypeStruct((b, l, nv), F32), jax.ShapeDtypeStruct((b, l, nv), F32),
                   jax.ShapeDtypeStruct((b,) + st_shape, F32), jax.ShapeDtypeStruct((b,) + st_shape, F32)),
        grid=(b, nb),
        in_specs=dir_specs(fwd_tok, 0) + dir_specs(rev_tok, 1) + [st_spec, st_spec],
        out_specs=(pl.BlockSpec((1, tb, nv), lambda bi, i: (bi, fwd_tok(i), 0)),
                   pl.BlockSpec((1, tb, nv), lambda bi, i: (bi, rev_tok(i), 0)),
                   st_spec, st_spec),
        scratch_shapes=[pltpu.VMEM(st_shape, F32), pltpu.VMEM(st_shape, F32)],
        compiler_params=_cparams(2),
        name="gla_scan",
    )(pb, pb, pb, lg, cos, sin, pb, pb, pb, lg, cos, sin, s0f, s0r)


def _rope_tables(seq_len):
    t = np.arange(seq_len)
    row = (t // GRID_W).astype(np.float32)
    col = (t % GRID_W).astype(np.float32)
    nf = GLA_DK // 4
    inv = np.float32(ROPE_BASE) ** (-np.arange(nf, dtype=np.float32) / np.float32(nf))
    ar = row[:, None] * inv[None, :]
    ac = col[:, None] * inv[None, :]
    cos = np.concatenate([np.cos(ar), np.cos(ar), np.cos(ac), np.cos(ac)], axis=1)
    sin = np.concatenate([-np.sin(ar), np.sin(ar), -np.sin(ac), np.sin(ac)], axis=1)
    return jnp.asarray(np.tile(cos, (1, 2)), F32), jnp.asarray(np.tile(sin, (1, 2)), F32)


def _combine_kernel(oa_ref, of_ref, ob_ref, gb_ref, gg_ref, w_ref, x_ref, g1_ref, o_ref, *, n_heads):
    ob = of_ref[0] + ob_ref[0]
    parts = []
    for h in range(n_heads):
        oh = ob[:, h * GLA_DV:(h + 1) * GLA_DV]
        ms = jnp.mean(oh * oh, axis=-1, keepdims=True)
        parts.append(oh * lax.rsqrt(ms + EPS))
    obn = jnp.concatenate(parts, axis=1) * gg_ref[...]
    gb = gb_ref[0]
    yb = obn * (gb / (1.0 + jnp.exp(-gb)))
    na_w = oa_ref.shape[2]
    y = _bdot(oa_ref[0], w_ref[0, 0:na_w, :]) + _bdot(yb.astype(BF16), w_ref[0, na_w:, :])
    o_ref[0] = x_ref[0] + g1_ref[0] * y


def _combine_call(oa, of, ob, pb, gla_g, w_out, layer, x, g1, tm):
    b, l, d = x.shape
    na_w = oa.shape[2]
    gl_w = of.shape[2]
    n_heads = gl_w // GLA_DV
    gate_blk = (pb.shape[2] - gl_w) // gl_w
    kern = functools.partial(_combine_kernel, n_heads=n_heads)
    return pl.pallas_call(
        kern,
        out_shape=jax.ShapeDtypeStruct((b, l, d), F32),
        grid=(b, l // tm),
        in_specs=[pl.BlockSpec((1, tm, na_w), lambda bi, i: (bi, i, 0)),
                  pl.BlockSpec((1, tm, gl_w), lambda bi, i: (bi, i, 0)),
                  pl.BlockSpec((1, tm, gl_w), lambda bi, i: (bi, i, 0)),
                  pl.BlockSpec((1, tm, gl_w), lambda bi, i: (bi, i, gate_blk)),
                  pl.BlockSpec((1, gl_w), lambda bi, i: (0, 0)),
                  pl.BlockSpec((1,) + w_out.shape[1:], lambda bi, i: (layer, 0, 0)),
                  pl.BlockSpec((1, tm, d), lambda bi, i: (bi, i, 0)),
                  pl.BlockSpec((1, 1, d), lambda bi, i: (bi, 0, 0))],
        out_specs=pl.BlockSpec((1, tm, d), lambda bi, i: (bi, i, 0)),
        compiler_params=_cparams(2),
        name="combine_out",
    )(oa, of, ob, pb, gla_g, w_out, x, g1)


def _pool_kernel(x_ref, xp_ref, xn_ref, g_ref, sh_ref, sc_ref, w_ref, ps_ref, g1_ref, o_ref, h_scr,
                 *, tm, seq_len):
    i = pl.program_id(1)
    n_i = pl.num_programs(1)
    g = g_ref[...]
    sh = sh_ref[0]
    sc = sc_ref[0]
    h_scr[HALO:HALO + tm, :] = _norm_mod(x_ref[0], g, sh, sc)
    h_scr[0:HALO, :] = jnp.where(i > 0, _norm_mod(xp_ref[0], g, sh, sc), 0.0)
    h_scr[HALO + tm:, :] = jnp.where(i < n_i - 1, _norm_mod(xn_ref[0], g, sh, sc), 0.0)

    t = i * tm + lax.broadcasted_iota(jnp.int32, (tm, 1), 0)
    n_ext = tm + 2 * HALO
    grp = h_scr.shape[1] // len(POOL_WINDOWS)
    ys = []
    for gi, w in enumerate(POOL_WINDOWS):
        parts = []
        for c0 in range(gi * grp, (gi + 1) * grp, 128):
            e = h_scr[:, c0:c0 + 128]
            acc = e + pltpu.roll(e, 1, axis=0)
            n = 2
            while n < w:
                acc = pltpu.roll(acc, n // 2, axis=0) + pltpu.roll(acc, n_ext - n // 2, axis=0)
                n *= 2
            parts.append(acc[HALO:HALO + tm])
        cols = slice(gi * grp, (gi + 1) * grp)
        cnt = (jnp.minimum(t + w // 2, seq_len) - jnp.maximum(t - w // 2, 0)).astype(F32)
        pooled = jnp.concatenate(parts, axis=1) / cnt - h_scr[HALO:HALO + tm, cols]
        ys.append(_bdot(pooled.astype(BF16), w_ref[0, gi]))
    y = jnp.concatenate(ys, axis=1) * ps_ref[...]
    o_ref[0] = x_ref[0] + g1_ref[0] * y


def _halo_specs(tm, l, d):
    per = tm // HALO
    last = l // HALO - 1
    prev = pl.BlockSpec((1, HALO, d), lambda bi, i, *_: (bi, jnp.maximum(i * per - 1, 0), 0))
    nxt = pl.BlockSpec((1, HALO, d), lambda bi, i, *_: (bi, jnp.minimum((i + 1) * per, last), 0))
    return prev, nxt


def _pool_call(x, g, shift, scale, pool_w, layer, pool_scale, g1, tm):
    b, l, d = x.shape
    prev, nxt = _halo_specs(tm, l, d)
    kern = functools.partial(_pool_kernel, tm=tm, seq_len=l)
    return pl.pallas_call(
        kern,
        out_shape=jax.ShapeDtypeStruct((b, l, d), F32),
        grid=(b, l // tm),
        in_specs=[pl.BlockSpec((1, tm, d), lambda bi, i: (bi, i, 0)), prev, nxt,
                  pl.BlockSpec((1, d), lambda bi, i: (0, 0)),
                  pl.BlockSpec((1, 1, d), lambda bi, i: (bi, 0, 0)),
                  pl.BlockSpec((1, 1, d), lambda bi, i: (bi, 0, 0)),
                  pl.BlockSpec((1,) + pool_w.shape[1:], lambda bi, i: (layer, 0, 0, 0)),
                  pl.BlockSpec((1, d), lambda bi, i: (0, 0)),
                  pl.BlockSpec((1, 1, d), lambda bi, i: (bi, 0, 0))],
        out_specs=pl.BlockSpec((1, tm, d), lambda bi, i: (bi, i, 0)),
        scratch_shapes=[pltpu.VMEM((tm + 2 * HALO, d), F32)],
        compiler_params=_cparams(2),
        name="pool_mixer",
    )(x, x, x, g, shift, scale, pool_w, pool_scale, g1)


def _ffn_kernel(x_ref, xp_ref, xn_ref, vec_ref, wv_ref, wg_ref, conv_ref, wd_ref, o_ref, h_scr, acc_scr,
                *, tm, seg, final_norm):
    i = pl.program_id(1)
    j = pl.program_id(2)

    @pl.when(j == 0)
    def _():
        g = vec_ref[0, 0:1, :]
        sh = vec_ref[0, 1:2, :]
        sc = vec_ref[0, 2:3, :]
        h_scr[HALO:HALO + tm, :] = _norm_mod(x_ref[0], g, sh, sc).astype(BF16)
        hp = _norm_mod(xp_ref[0], g, sh, sc)
        hn = _norm_mod(xn_ref[0], g, sh, sc)
        if seg % tm == 0:
            hp = jnp.where(lax.rem(i * tm, seg) == 0, 0.0, hp)
            hn = jnp.where(lax.rem((i + 1) * tm, seg) == 0, 0.0, hn)
        h_scr[0:HALO, :] = hp.astype(BF16)
        h_scr[HALO + tm:, :] = hn.astype(BF16)
        acc_scr[...] = jnp.zeros_like(acc_scr)

    rows_all = tm + 2 * HALO
    ug = _bdot(h_scr[...], wg_ref[0])
    uv = _bdot(h_scr[HALO:HALO + tm, :], wv_ref[0])
    cw = conv_ref[0]
    g_prev = pltpu.roll(ug, 1, axis=0)[HALO:HALO + tm]
    g_next = pltpu.roll(ug, rows_all - 1, axis=0)[HALO:HALO + tm]
    if seg % tm != 0:
        pos = lax.rem(i * tm + lax.broadcasted_iota(jnp.int32, (tm, 1), 0), seg)
        g_prev = jnp.where(pos == 0, 0.0, g_prev)
        g_next = jnp.where(pos == seg - 1, 0.0, g_next)
    gate = g_prev * cw[0:1] + ug[HALO:HALO + tm] * cw[1:2] + g_next * cw[2:3] + cw[3:4]
    act = 0.5 * gate * (1.0 + lax.erf(gate * (2.0 ** -0.5))) * uv
    acc_scr[...] += _bdot(act.astype(BF16), wd_ref[0].astype(BF16))

    @pl.when(j == pl.num_programs(2) - 1)
    def _():
        y = x_ref[0] + vec_ref[0, 3:4, :] * acc_scr[...]
        if final_norm:
            ms = jnp.mean(y * y, axis=-1, keepdims=True)
            y = y * lax.rsqrt(ms + EPS) * vec_ref[0, 4:5, :]
        o_ref[0] = y


def _ffn_call(x, g, shift, scale, w_up, conv, w_down, layer, g2, final_g, tm, tf, seg, final_norm):
    b, l, d = x.shape
    nf = w_down.shape[1] // tf
    rows = [jnp.broadcast_to(g, (b, 1, d)), shift, scale, g2, jnp.broadcast_to(final_g, (b, 1, d)),
            jnp.zeros((b, 3, d), F32)]
    vec = jnp.concatenate(rows, axis=1)
    prev, nxt = _halo_specs(tm, l, d)
    kern = functools.partial(_ffn_kernel, tm=tm, seg=seg, final_norm=final_norm)
    return pl.pallas_call(
        kern,
        out_shape=jax.ShapeDtypeStruct((b, l, d), F32),
        grid=(b, l // tm, nf),
        in_specs=[pl.BlockSpec((1, tm, d), lambda bi, i, j: (bi, i, 0)), prev, nxt,
                  pl.BlockSpec((1, 8, d), lambda bi, i, j: (bi, 0, 0)),
                  pl.BlockSpec((1, d, tf), lambda bi, i, j: (layer, 0, j)),
                  pl.BlockSpec((1, d, tf), lambda bi, i, j: (layer, 0, nf + j)),
                  pl.BlockSpec((1, 4, tf), lambda bi, i, j: (layer, 0, j)),
                  pl.BlockSpec((1, tf, d), lambda bi, i, j: (layer, j, 0))],
        out_specs=pl.BlockSpec((1, tm, d), lambda bi, i, j: (bi, i, 0)),
        scratch_shapes=[pltpu.VMEM((tm + 2 * HALO, d), BF16), pltpu.VMEM((tm, d), F32)],
        compiler_params=_cparams(3),
        name="conv_ffn",
    )(x, x, x, vec, w_up, w_up, conv, w_down)


def _tile(l, want):
    return min(l, want)


def _even_mixer(x_lat, x_ctx, mods_lat, mods_ctx, n1, e, w_in, w_in_bf, w_gate2, b_gate, rpb, gla_g, w_out_bf,
                need_ctx, rope_lat, rope_ctx):
    sh, sc, g1 = mods_lat
    shc, scc, gc1 = mods_ctx
    nh = rpb.shape[0]
    a_w = 3 * nh * HEAD_DIM
    n_gla = w_gate2.shape[2] // GLA_DK
    b_w = 2 * n_gla * GLA_DK + 2 * n_gla * GLA_DV
    w_ab = jnp.pad(w_in[:, a_w + b_w:], ((0, 0), (0, 128 - 2 * GLA_RANK))).astype(BF16)
    nqk = n_gla * GLA_DK
    wg2 = jnp.zeros((128, 2 * nqk), F32)
    wg2 = wg2.at[0:GLA_RANK, 0:nqk].set(w_gate2[0]).at[GLA_RANK:2 * GLA_RANK, nqk:].set(w_gate2[1]).astype(BF16)
    bg = b_gate.reshape(1, 2 * nqk)

    l = x_lat.shape[1]
    lc = x_ctx.shape[1]
    bsz, d = x_lat.shape[0], x_lat.shape[2]
    pa, pb, lg = _proj_call(x_lat, n1, sh, sc, w_in_bf, e, a_w, b_w, w_ab, wg2, bg, _tile(l, 1024), 512)
    ctx_flat = x_ctx.reshape(1, bsz * lc, d)
    pa_c, pb_c, lg_c = [a.reshape(bsz, lc, -1) for a in
                        _proj_call(ctx_flat, n1, shc[:1], scc[:1], w_in_bf, e, a_w, b_w, w_ab, wg2, bg,
                                   _tile(bsz * lc, 1024), 512)]

    oa = _na_call(pa, pa_c, rpb)

    s0 = jnp.zeros((bsz, nqk // 128, 2 * GLA_DV, 2 * GLA_DK), F32)
    of_c, ob_c, s_f, s_b = _gla_call(pb_c, lg_c, rope_ctx[0], rope_ctx[1], s0, s0, tb=_tile(lc, 512))
    of, ob, _, _ = _gla_call(pb, lg, rope_lat[0], rope_lat[1], s_f, s_b, tb=_tile(l, 512))

    gg = gla_g.reshape(1, -1)
    x_lat = _combine_call(oa, of, ob, pb, gg, w_out_bf, e, x_lat, g1, _tile(l, 512))
    if need_ctx:
        oa_c = _ctx_attn_call(pa_c, nh)
        x_ctx = _combine_call(oa_c, of_c, ob_c, pb_c, gg, w_out_bf, e, x_ctx, gc1, _tile(lc, 512))
    return x_lat, x_ctx


def kernel(x, c, ctx, c_ctx, w_mod, b_mod, norm1_g, norm2_g, w_in, w_gate2, b_gate, rpb, gla_norm_g, w_out,
           pool_w, pool_scale, w_up, conv_w, conv_b, w_down, final_g):
    bsz, seq, d = x.shape
    lc = ctx.shape[1]
    depth = w_mod.shape[0]

    cvec = jnp.zeros((8, d), F32).at[0:bsz].set(c).at[bsz].set(c_ctx)
    mods = _mod_call(cvec, w_mod, b_mod)

    rope_lat = _rope_tables(seq)
    rope_ctx = (jnp.ones((lc, 128), F32), jnp.zeros((lc, 128), F32))

    fg = final_g.reshape(1, d)
    w_in_bf = w_in[:, :, :w_in.shape[2] // 512 * 512].astype(BF16)
    w_out_bf = w_out.astype(BF16)
    w_up_bf = w_up.astype(BF16)
    pool_w_bf = pool_w.astype(BF16)
    conv = jnp.concatenate([conv_w, conv_b[:, None, :]], axis=1)
    x_lat, x_ctx = x, ctx
    for i in range(depth):
        is_even = i % 2 == 0
        need_ctx = any(j % 2 == 0 for j in range(i + 1, depth))
        m = mods[i].reshape(8, 6, d)
        lat = [m[0:bsz, k][:, None, :] for k in range(6)]
        cx = [jnp.broadcast_to(m[bsz, k][None, None, :], (bsz, 1, d)) for k in range(6)]
        n1 = norm1_g[i].reshape(1, d)
        n2 = norm2_g[i].reshape(1, d)
        if is_even:
            e = i // 2
            x_lat, x_ctx = _even_mixer(x_lat, x_ctx, lat[0:3], cx[0:3], n1, e, w_in[e], w_in_bf, w_gate2[e], b_gate[e],
                                       rpb[e], gla_norm_g[e], w_out_bf, need_ctx, rope_lat, rope_ctx)
        else:
            o = i // 2
            ps = pool_scale[o].reshape(1, d)
            x_lat = _pool_call(x_lat, n1, lat[0], lat[1], pool_w_bf, o, ps, lat[2], _tile(seq, 512))
            if need_ctx:
                x_ctx = _pool_call(x_ctx, n1, cx[0], cx[1], pool_w_bf, o, ps, cx[2], _tile(lc, 512))
        x_lat = _ffn_call(x_lat, n2, lat[3], lat[4], w_up_bf, conv, w_down, i, lat[5], fg,
                          _tile(seq, 512), 512, seg=seq, final_norm=(i == depth - 1))
        if need_ctx:
            ctx_flat = x_ctx.reshape(1, bsz * lc, d)
            x_ctx = _ffn_call(ctx_flat, n2, cx[3][:1], cx[4][:1], w_up_bf, conv, w_down, i, cx[5][:1], fg,
                              _tile(bsz * lc, 512), 512, seg=lc, final_norm=False).reshape(bsz, lc, d)
    return x_lat
```

```python
import functools

import numpy as np
import jax
import jax.numpy as jnp
from jax import lax
from jax.experimental import pallas as pl
from jax.experimental.pallas import tpu as pltpu

F32 = jnp.float32
BF16 = jnp.bfloat16

GRID_W = 64
HEAD_DIM = 128
NA_ROWS = 8
NA_COLS = 16
NA_GROUP = 4
GLA_DK = 64
GLA_DV = 128
GLA_RANK = 16
GLA_TAU = 16.0
GLA_CHUNK = 64
POOL_WINDOWS = (2, 4, 8, 16)
ROPE_BASE = 10000.0
EPS = 1e-6
LOG2E = 1.4426950408889634
HALO = 16
VMEM_LIMIT = 56 * 1024 * 1024


def _cparams(n_axes):
    return pltpu.CompilerParams(dimension_semantics=("arbitrary",) * n_axes,
                                vmem_limit_bytes=VMEM_LIMIT)


def _bdot(a, b):
    return jnp.dot(a, b, preferred_element_type=F32)


def _bdot_nt(a, b):
    return lax.dot_general(a, b, (((1,), (1,)), ((), ())), preferred_element_type=F32)


def _norm_mod(x, g, shift, scale):
    ms = jnp.mean(x * x, axis=-1, keepdims=True)
    return x * lax.rsqrt(ms + EPS) * (g * (1.0 + scale)) + shift


def _mod_kernel(c_ref, w_ref, b_ref, o_ref):
    c = c_ref[...]
    s = c / (1.0 + jnp.exp(-c))
    o_ref[0] = _bdot(s.astype(BF16), w_ref[0].astype(BF16)) + b_ref[0]


def _mod_call(cvec, w_mod, b_mod):
    depth, d, n = w_mod.shape
    tn = 1024
    return pl.pallas_call(
        _mod_kernel,
        out_shape=jax.ShapeDtypeStruct((depth, 8, n), F32),
        grid=(depth, n // tn),
        in_specs=[pl.BlockSpec((8, d), lambda l, j: (0, 0)),
                  pl.BlockSpec((1, d, tn), lambda l, j: (l, 0, j)),
                  pl.BlockSpec((1, 1, tn), lambda l, j: (l, 0, j))],
        out_specs=pl.BlockSpec((1, 8, tn), lambda l, j: (l, 0, j)),
        compiler_params=_cparams(2),
        name="mod_matvec",
    )(cvec, w_mod, b_mod.reshape(depth, 1, n))


def _proj_kernel(x_ref, g_ref, sh_ref, sc_ref, wa_ref, wb_ref, wab_ref, wg2_ref, bg_ref,
                 oa_ref, ob_ref, lg_ref, h_scr, *, na_tiles):
    j = pl.program_id(2)

    @pl.when(j == 0)
    def _():
        h = _norm_mod(x_ref[0], g_ref[...], sh_ref[0], sc_ref[0]).astype(BF16)
        h_scr[...] = h
        ab = _bdot(h, wab_ref[...])
        z = _bdot(ab.astype(BF16), wg2_ref[...]) + bg_ref[...]
        log_sig = jnp.minimum(z, 0.0) - jnp.log1p(jnp.exp(-jnp.abs(z)))
        lg_ref[0] = log_sig * (1.0 / GLA_TAU)

    @pl.when(j < na_tiles)
    def _():
        oa_ref[0] = _bdot(h_scr[...], wa_ref[0]).astype(oa_ref.dtype)

    @pl.when(j >= na_tiles)
    def _():
        ob_ref[0] = _bdot(h_scr[...], wb_ref[0])


def _proj_call(x, g, shift, scale, w_in, layer, na, nb_, wab, wg2, bg, tm, tn):
    b, l, d = x.shape
    ta, tb_ = na // tn, nb_ // tn
    ng = wg2.shape[1]
    kern = functools.partial(_proj_kernel, na_tiles=ta)
    return pl.pallas_call(
        kern,
        out_shape=(jax.ShapeDtypeStruct((b, l, na), BF16), jax.ShapeDtypeStruct((b, l, nb_), F32),
                   jax.ShapeDtypeStruct((b, l, ng), F32)),
        grid=(b, l // tm, ta + tb_),
        in_specs=[pl.BlockSpec((1, tm, d), lambda bi, i, j: (bi, i, 0)),
                  pl.BlockSpec((1, d), lambda bi, i, j: (0, 0)),
                  pl.BlockSpec((1, 1, d), lambda bi, i, j: (bi, 0, 0)),
                  pl.BlockSpec((1, 1, d), lambda bi, i, j: (bi, 0, 0)),
                  pl.BlockSpec((1, d, tn), lambda bi, i, j: (layer, 0, jnp.minimum(j, ta - 1))),
                  pl.BlockSpec((1, d, tn), lambda bi, i, j: (layer, 0, jnp.maximum(j, ta))),
                  pl.BlockSpec(wab.shape, lambda bi, i, j: (0, 0)),
                  pl.BlockSpec(wg2.shape, lambda bi, i, j: (0, 0)),
                  pl.BlockSpec((1, ng), lambda bi, i, j: (0, 0))],
        out_specs=(pl.BlockSpec((1, tm, tn), lambda bi, i, j: (bi, i, jnp.minimum(j, ta - 1))),
                   pl.BlockSpec((1, tm, tn), lambda bi, i, j: (bi, i, jnp.maximum(j - ta, 0))),
                   pl.BlockSpec((1, tm, ng), lambda bi, i, j: (bi, i, 0))),
        scratch_shapes=[pltpu.VMEM((tm, d), BF16)],
        compiler_params=_cparams(3),
        name="proj_in",
    )(x, g, shift, scale, w_in, w_in, wab, wg2, bg)


def _na_kernel(rpb_ref, q_ref, k_ref, v_ref, kc_ref, vc_ref, o_ref, tab_ref, *, rows, n_dr, n_dc):
    head = pl.program_id(0)
    scale = HEAD_DIM ** -0.5 * LOG2E
    wr = NA_ROWS

    qc = lax.broadcasted_iota(jnp.int32, (GRID_W, 2 * GRID_W), 0)
    lane = lax.broadcasted_iota(jnp.int32, (GRID_W, 2 * GRID_W), 1)
    kc = lane & (GRID_W - 1)
    dc = jnp.clip(kc - qc + (NA_COLS - 1), 0, n_dc - 1)
    cstart = jnp.clip(qc - NA_COLS // 2, 0, GRID_W - NA_COLS)
    in_win = (kc >= cstart) & (kc < cstart + NA_COLS)
    left = lane < GRID_W
    base_off = head * (n_dr * n_dc)

    @pl.when(pl.program_id(1) == 0)
    def _():
        for d in range(-1, n_dr):
            acc = jnp.zeros((GRID_W, 2 * GRID_W), F32)
            for c in range(n_dc):
                if d < 0:
                    val = rpb_ref[base_off + (d + 1) * n_dc + c]
                elif d + 1 >= n_dr:
                    val = rpb_ref[base_off + d * n_dc + c]
                else:
                    val = jnp.where(left, rpb_ref[base_off + d * n_dc + c], rpb_ref[base_off + (d + 1) * n_dc + c])
                acc = jnp.where(dc == c, val, acc)
            keep = in_win & ~left if d < 0 else (in_win & left if d + 1 >= n_dr else in_win)
            tab_ref[d + 1] = jnp.where(keep, acc * LOG2E, -jnp.inf)

    kctx = kc_ref[0]
    vctx = vc_ref[0]
    grp = NA_GROUP
    union = grp + wr
    neg_inf = jnp.full((GRID_W, 2 * GRID_W), -jnp.inf, F32)

    def scores(gi):
        r0 = gi * grp
        ustart = jnp.clip(r0 - wr // 2, 0, rows - union)
        q_rows = pl.ds(pl.multiple_of(r0 * GRID_W, grp * GRID_W), grp * GRID_W)
        kv_rows = pl.ds(pl.multiple_of(ustart * GRID_W, GRID_W), union * GRID_W)
        q = q_ref[0, q_rows, :]
        kb = k_ref[0, kv_rows, :]
        biases = []
        for g in range(grp):
            rq = r0 + g
            rstart = jnp.clip(rq - wr // 2, 0, rows - wr)
            tiles = []
            for j in range(union // 2):
                ku = ustart + 2 * j
                d = ku - rq + (NA_ROWS - 1)
                ok_l = ((ku >= rstart) & (ku < rstart + wr)).astype(jnp.int32)
                ok_r = ((ku + 1 >= rstart) & (ku + 1 < rstart + wr)).astype(jnp.int32)
                tile = tab_ref[jnp.clip(d, -1, n_dr - 1) + 1]
                tiles.append(jnp.where(jnp.where(left, ok_l, ok_r) > 0, tile, neg_inf))
            biases.append(jnp.concatenate(tiles, axis=1))
        s_loc = _bdot_nt(q, kb) * scale + jnp.concatenate(biases, axis=0)
        s_ctx = _bdot_nt(q, kctx) * scale
        return s_loc, s_ctx, q_rows, kv_rows

    def softmax(s_loc, s_ctx):
        m = jnp.maximum(jnp.max(s_loc, axis=-1, keepdims=True), jnp.max(s_ctx, axis=-1, keepdims=True))
        p_loc = jnp.exp2(s_loc - m)
        p_ctx = jnp.exp2(s_ctx - m)
        denom = jnp.sum(p_loc, axis=-1, keepdims=True) + jnp.sum(p_ctx, axis=-1, keepdims=True)
        return p_loc.astype(BF16), p_ctx.astype(BF16), denom

    def values(p_loc, p_ctx, denom, q_rows, kv_rows):
        o = _bdot(p_loc, v_ref[0, kv_rows, :]) + _bdot(p_ctx, vctx)
        o_ref[0, q_rows, :] = (o / denom).astype(o_ref.dtype)

    def group_pair(t, carry):
        sa = scores(2 * t)
        sb = scores(2 * t + 1)
        pa = softmax(sa[0], sa[1])
        pb = softmax(sb[0], sb[1])
        values(*pa, sa[2], sa[3])
        values(*pb, sb[2], sb[3])
        return carry

    lax.fori_loop(0, rows // (2 * grp), group_pair, 0)


def _na_call(pa, pa_ctx, rpb):
    b, l, _ = pa.shape
    lc = pa_ctx.shape[1]
    nh, n_dr, n_dc = rpb.shape
    rows = l // GRID_W
    kern = functools.partial(_na_kernel, rows=rows, n_dr=n_dr, n_dc=n_dc)
    return pl.pallas_call(
        kern,
        out_shape=jax.ShapeDtypeStruct((b, l, nh * HEAD_DIM), BF16),
        grid=(nh, b),
        in_specs=[pl.BlockSpec(memory_space=pltpu.SMEM),
                  pl.BlockSpec((1, l, HEAD_DIM), lambda h, bi: (bi, 0, h)),
                  pl.BlockSpec((1, l, HEAD_DIM), lambda h, bi: (bi, 0, nh + h)),
                  pl.BlockSpec((1, l, HEAD_DIM), lambda h, bi: (bi, 0, 2 * nh + h)),
                  pl.BlockSpec((1, lc, HEAD_DIM), lambda h, bi: (bi, 0, nh + h)),
                  pl.BlockSpec((1, lc, HEAD_DIM), lambda h, bi: (bi, 0, 2 * nh + h))],
        out_specs=pl.BlockSpec((1, l, HEAD_DIM), lambda h, bi: (bi, 0, h)),
        scratch_shapes=[pltpu.VMEM((n_dr + 1, GRID_W, 2 * GRID_W), F32)],
        compiler_params=_cparams(2),
        name="na_attention",
    )(rpb.reshape(-1), pa, pa, pa, pa_ctx, pa_ctx)


def _ctx_attn_kernel(q_ref, k_ref, v_ref, o_ref):
    q = q_ref[0]
    s = _bdot_nt(q, k_ref[0]) * (HEAD_DIM ** -0.5)
    m = jnp.max(s, axis=-1, keepdims=True)
    p = jnp.exp(s - m)
    denom = jnp.sum(p, axis=-1, keepdims=True)
    o_ref[0] = (_bdot(p.astype(BF16), v_ref[0]) / denom).astype(o_ref.dtype)


def _ctx_attn_call(pa_ctx, nh):
    b, lc, _ = pa_ctx.shape
    return pl.pallas_call(
        _ctx_attn_kernel,
        out_shape=jax.ShapeDtypeStruct((b, lc, nh * HEAD_DIM), BF16),
        grid=(b, nh),
        in_specs=[pl.BlockSpec((1, lc, HEAD_DIM), lambda bi, h: (bi, 0, h)),
                  pl.BlockSpec((1, lc, HEAD_DIM), lambda bi, h: (bi, 0, nh + h)),
                  pl.BlockSpec((1, lc, HEAD_DIM), lambda bi, h: (bi, 0, 2 * nh + h))],
        out_specs=pl.BlockSpec((1, lc, HEAD_DIM), lambda bi, h: (bi, 0, h)),
        compiler_params=_cparams(2),
        name="ctx_attention",
    )(pa_ctx, pa_ctx, pa_ctx)


_GLA_LEVELS = (32, 16, 8, 4, 2, 1)


def _gla_dir_consts(rev):
    c = GLA_CHUNK
    ii = lax.broadcasted_iota(jnp.int32, (c, c), 0)
    jj = lax.broadcasted_iota(jnp.int32, (c, c), 1)
    tri = ((jj >= ii) if rev else (jj <= ii)).astype(BF16)
    i2 = lax.broadcasted_iota(jnp.int32, (c, 2 * c), 0)
    j2 = lax.broadcasted_iota(jnp.int32, (c, 2 * c), 1) & (c - 1)
    masks = [i2 == j2]
    for s in _GLA_LEVELS:
        same = (i2 // (2 * s)) == (j2 // (2 * s))
        qi = i2 % (2 * s)
        kj = j2 % (2 * s)
        masks.append(same & ((qi < s) & (kj >= s) if rev else (qi >= s) & (kj < s)))
    return tri, masks


def _gla_ref_rows(cum, s, rev, sub8):
    c, w = cum.shape
    blk = 2 * s
    off = s - 1 if rev else s
    pieces = []
    if blk >= 8:
        for b0 in range(0, c, blk):
            pieces.append(jnp.broadcast_to(cum[b0 + off:b0 + off + 1, :], (blk, w)))
    else:
        for g0 in range(0, c, 8):
            out = None
            for b0 in range(0, 8, blk):
                cand = jnp.broadcast_to(cum[g0 + b0 + off:g0 + b0 + off + 1, :], (8, w))
                out = cand if out is None else jnp.where(sub8 >= b0, cand, out)
            pieces.append(out)
    return jnp.concatenate(pieces, axis=0)


def _gla_prep(rev, tri, sub8, q, k, g):
    c = GLA_CHUNK
    g1 = g.astype(BF16)
    r1 = g - g1.astype(F32)
    g2 = r1.astype(BF16)
    g3 = (r1 - g2.astype(F32)).astype(BF16)
    cum = (_bdot(tri, g1) + _bdot(tri, g2) + _bdot(tri, g3)) * LOG2E
    end_row = 0 if rev else c - 1
    last = cum[end_row:end_row + 1, :]
    q_in = (q * jnp.exp2(cum)).astype(BF16)
    k_out = (k * jnp.exp2(last - cum)).astype(BF16)
    e_last = jnp.exp2(last)

    qs = [q.astype(BF16)]
    ks = [k.astype(BF16)]
    for s in _GLA_LEVELS:
        dq = cum - _gla_ref_rows(cum, s, rev, sub8)
        e = jnp.exp2(jnp.minimum(dq, -dq))
        qs.append((q * e).astype(BF16))
        ks.append((k * e).astype(BF16))
    return qs, ks, q_in, k_out, e_last


def _gla_scores(masks, left128, qs, ks, n_pairs):
    c = GLA_CHUNK
    attns = [jnp.zeros((c, 2 * c), F32) for _ in range(n_pairs)]
    for lv_i in range(len(masks)):
        for p in range(n_pairs):
            ksl = slice(p * 128, (p + 1) * 128)
            kp = ks[lv_i][:, ksl]
            kbd = jnp.concatenate([jnp.where(left128, kp, 0), jnp.where(left128, 0, kp)], axis=0)
            attns[p] = jnp.where(masks[lv_i], _bdot_nt(qs[lv_i][:, ksl], kbd), attns[p])
    return [a.astype(BF16) for a in attns]


def _gla_finish(left256, st_diag, attns, q_in, k_out, e_last, v, st_ref, n_pairs):
    outs = []
    for p in range(n_pairs):
        ksl = slice(p * 128, (p + 1) * 128)
        vp16 = v[:, p * 256:(p + 1) * 256].astype(BF16)
        vbd = jnp.concatenate([jnp.where(left256, vp16, 0), jnp.where(left256, 0, vp16)], axis=0)
        st = st_ref[p]
        outs.append(_bdot(attns[p], vbd) + _bdot_nt(q_in[:, ksl], st.astype(BF16)))
        upd = lax.dot_general(vp16, k_out[:, ksl], (((0,), (0,)), ((), ())), preferred_element_type=F32)
        st_ref[p] = st * e_last[:, ksl] + jnp.where(st_diag, upd, 0.0)
    return jnp.concatenate(outs, axis=1)


def _gla_kernel(qf_ref, kf_ref, vf_ref, gf_ref, cosf_ref, sinf_ref,
                qr_ref, kr_ref, vr_ref, gr_ref, cosr_ref, sinr_ref, s0f_ref, s0r_ref,
                of_ref, or_ref, sff_ref, sfr_ref, stf_scr, str_scr, *, n_chunks, n_pairs):
    c = GLA_CHUNK
    blk_i = pl.program_id(1)

    @pl.when(blk_i == 0)
    def _():
        stf_scr[...] = s0f_ref[0]
        str_scr[...] = s0r_ref[0]

    l2 = lax.broadcasted_iota(jnp.int32, (c, 2 * c), 1)
    left128 = l2 < c
    lane32 = l2 & 31
    left256 = lax.broadcasted_iota(jnp.int32, (c, 2 * GLA_DV), 1) < GLA_DV
    sr = lax.broadcasted_iota(jnp.int32, (2 * GLA_DV, 2 * GLA_DK), 0)
    sc_ = lax.broadcasted_iota(jnp.int32, (2 * GLA_DV, 2 * GLA_DK), 1)
    st_diag = (sr // GLA_DV) == (sc_ // GLA_DK)
    sub8 = lax.broadcasted_iota(jnp.int32, (8, n_pairs * 128), 0)
    tri_f, masks_f = _gla_dir_consts(False)
    tri_r, masks_r = _gla_dir_consts(True)

    def rope(x, cos, sin):
        outs = []
        for p in range(n_pairs):
            xs = x[:, p * 128:(p + 1) * 128]
            up = pltpu.roll(xs, 16, axis=1)
            dn = pltpu.roll(xs, 112, axis=1)
            sw = jnp.where(lane32 < 16, dn, up)
            outs.append(xs * cos + sw * sin)
        return jnp.concatenate(outs, axis=1)

    def prep(rev, tri, rows, q_ref, k_ref, g_ref, cos_ref, sin_ref):
        cos = cos_ref[rows, :]
        sin = sin_ref[rows, :]
        q = rope(q_ref[0, rows, :], cos, sin) * (GLA_DK ** -0.5)
        k = rope(k_ref[0, rows, :], cos, sin)
        return _gla_prep(rev, tri, sub8, q, k, g_ref[0, rows, :])

    def chunk(ci, carry):
        rows_f = pl.ds(pl.multiple_of(ci * c, c), c)
        rows_r = pl.ds(pl.multiple_of((n_chunks - 1 - ci) * c, c), c)
        qs_f, ks_f, qin_f, kout_f, el_f = prep(False, tri_f, rows_f, qf_ref, kf_ref, gf_ref, cosf_ref, sinf_ref)
        qs_r, ks_r, qin_r, kout_r, el_r = prep(True, tri_r, rows_r, qr_ref, kr_ref, gr_ref, cosr_ref, sinr_ref)
        at_f = _gla_scores(masks_f, left128, qs_f, ks_f, n_pairs)
        at_r = _gla_scores(masks_r, left128, qs_r, ks_r, n_pairs)
        of_ref[0, rows_f, :] = _gla_finish(left256, st_diag, at_f, qin_f, kout_f, el_f, vf_ref[0, rows_f, :],
                                           stf_scr, n_pairs)
        or_ref[0, rows_r, :] = _gla_finish(left256, st_diag, at_r, qin_r, kout_r, el_r, vr_ref[0, rows_r, :],
                                           str_scr, n_pairs)
        return carry

    lax.fori_loop(0, n_chunks, chunk, 0)

    @pl.when(blk_i == pl.num_programs(1) - 1)
    def _():
        sff_ref[0] = stf_scr[...]
        sfr_ref[0] = str_scr[...]


def _gla_call(pb, lg, cos, sin, s0f, s0r, *, tb):
    b, l, _ = pb.shape
    nqk = lg.shape[2] // 2
    n_pairs = nqk // 128
    nv = n_pairs * 2 * GLA_DV
    nb = l // tb
    kern = functools.partial(_gla_kernel, n_chunks=tb // GLA_CHUNK, n_pairs=n_pairs)
    st_shape = (n_pairs, 2 * GLA_DV, 2 * GLA_DK)

    def dir_specs(tok, gate_blk):
        return [pl.BlockSpec((1, tb, nqk), lambda bi, i: (bi, tok(i), 0)),
                pl.BlockSpec((1, tb, nqk), lambda bi, i: (bi, tok(i), 1)),
                pl.BlockSpec((1, tb, nv), lambda bi, i: (bi, tok(i), 1)),
                pl.BlockSpec((1, tb, nqk), lambda bi, i: (bi, tok(i), gate_blk)),
                pl.BlockSpec((tb, 128), lambda bi, i: (tok(i), 0)),
                pl.BlockSpec((tb, 128), lambda bi, i: (tok(i), 0))]

    fwd_tok = lambda i: i
    rev_tok = lambda i: nb - 1 - i
    st_spec = pl.BlockSpec((1,) + st_shape, lambda bi, i: (bi, 0, 0, 0))
    return pl.pallas_call(
        kern,
        out_shape=(jax.ShapeDtypeStruct((b, l, nv), F32), jax.ShapeDtypeStruct((b, l, nv), F32),
                   jax.ShapeDtypeStruct((b,) + st_shape, F32), jax.ShapeDtypeStruct((b,) + st_shape, F32)),
        grid=(b, nb),
        in_specs=dir_specs(fwd_tok, 0) + dir_specs(rev_tok, 1) + [st_spec, st_spec],
        out_specs=(pl.BlockSpec((1, tb, nv), lambda bi, i: (bi, fwd_tok(i), 0)),
                   pl.BlockSpec((1, tb, nv), lambda bi, i: (bi, rev_tok(i), 0)),
                   st_spec, st_spec),
        scratch_shapes=[pltpu.VMEM(st_shape, F32), pltpu.VMEM(st_shape, F32)],
        compiler_params=_cparams(2),
        name="gla_scan",
    )(pb, pb, pb, lg, cos, sin, pb, pb, pb, lg, cos, sin, s0f, s0r)


def _rope_tables(seq_len):
    t = np.arange(seq_len)
    row = (t // GRID_W).astype(np.float32)
    col = (t % GRID_W).astype(np.float32)
    nf = GLA_DK // 4
    inv = np.float32(ROPE_BASE) ** (-np.arange(nf, dtype=np.float32) / np.float32(nf))
    ar = row[:, None] * inv[None, :]
    ac = col[:, None] * inv[None, :]
    cos = np.concatenate([np.cos(ar), np.cos(ar), np.cos(ac), np.cos(ac)], axis=1)
    sin = np.concatenate([-np.sin(ar), np.sin(ar), -np.sin(ac), np.sin(ac)], axis=1)
    return jnp.asarray(np.tile(cos, (1, 2)), F32), jnp.asarray(np.tile(sin, (1, 2)), F32)


def _combine_kernel(oa_ref, of_ref, ob_ref, gb_ref, gg_ref, w_ref, x_ref, g1_ref, o_ref, *, n_heads):
    ob = of_ref[0] + ob_ref[0]
    parts = []
    for h in range(n_heads):
        oh = ob[:, h * GLA_DV:(h + 1) * GLA_DV]
        ms = jnp.mean(oh * oh, axis=-1, keepdims=True)
        parts.append(oh * lax.rsqrt(ms + EPS))
    obn = jnp.concatenate(parts, axis=1) * gg_ref[...]
    gb = gb_ref[0]
    yb = obn * (gb / (1.0 + jnp.exp(-gb)))
    na_w = oa_ref.shape[2]
    y = _bdot(oa_ref[0], w_ref[0, 0:na_w, :]) + _bdot(yb.astype(BF16), w_ref[0, na_w:, :])
    o_ref[0] = x_ref[0] + g1_ref[0] * y


def _combine_call(oa, of, ob, pb, gla_g, w_out, layer, x, g1, tm):
    b, l, d = x.shape
    na_w = oa.shape[2]
    gl_w = of.shape[2]
    n_heads = gl_w // GLA_DV
    gate_blk = (pb.shape[2] - gl_w) // gl_w
    kern = functools.partial(_combine_kernel, n_heads=n_heads)
    return pl.pallas_call(
        kern,
        out_shape=jax.ShapeDtypeStruct((b, l, d), F32),
        grid=(b, l // tm),
        in_specs=[pl.BlockSpec((1, tm, na_w), lambda bi, i: (bi, i, 0)),
                  pl.BlockSpec((1, tm, gl_w), lambda bi, i: (bi, i, 0)),
                  pl.BlockSpec((1, tm, gl_w), lambda bi, i: (bi, i, 0)),
                  pl.BlockSpec((1, tm, gl_w), lambda bi, i: (bi, i, gate_blk)),
                  pl.BlockSpec((1, gl_w), lambda bi, i: (0, 0)),
                  pl.BlockSpec((1,) + w_out.shape[1:], lambda bi, i: (layer, 0, 0)),
                  pl.BlockSpec((1, tm, d), lambda bi, i: (bi, i, 0)),
                  pl.BlockSpec((1, 1, d), lambda bi, i: (bi, 0, 0))],
        out_specs=pl.BlockSpec((1, tm, d), lambda bi, i: (bi, i, 0)),
        compiler_params=_cparams(2),
        name="combine_out",
    )(oa, of, ob, pb, gla_g, w_out, x, g1)


def _pool_kernel(x_ref, xp_ref, xn_ref, g_ref, sh_ref, sc_ref, w_ref, ps_ref, g1_ref, o_ref, h_scr,
                 *, tm, seq_len):
    i = pl.program_id(1)
    n_i = pl.num_programs(1)
    g = g_ref[...]
    sh = sh_ref[0]
    sc = sc_ref[0]
    h_scr[HALO:HALO + tm, :] = _norm_mod(x_ref[0], g, sh, sc)
    h_scr[0:HALO, :] = jnp.where(i > 0, _norm_mod(xp_ref[0], g, sh, sc), 0.0)
    h_scr[HALO + tm:, :] = jnp.where(i < n_i - 1, _norm_mod(xn_ref[0], g, sh, sc), 0.0)

    t = i * tm + lax.broadcasted_iota(jnp.int32, (tm, 1), 0)
    n_ext = tm + 2 * HALO
    grp = h_scr.shape[1] // len(POOL_WINDOWS)
    ys = []
    for gi, w in enumerate(POOL_WINDOWS):
        parts = []
        for c0 in range(gi * grp, (gi + 1) * grp, 128):
            e = h_scr[:, c0:c0 + 128]
            acc = e + pltpu.roll(e, 1, axis=0)
            n = 2
            while n < w:
                acc = pltpu.roll(acc, n // 2, axis=0) + pltpu.roll(acc, n_ext - n // 2, axis=0)
                n *= 2
            parts.append(acc[HALO:HALO + tm])
        cols = slice(gi * grp, (gi + 1) * grp)
        cnt = (jnp.minimum(t + w // 2, seq_len) - jnp.maximum(t - w // 2, 0)).astype(F32)
        pooled = jnp.concatenate(parts, axis=1) / cnt - h_scr[HALO:HALO + tm, cols]
        ys.append(_bdot(pooled.astype(BF16), w_ref[0, gi]))
    y = jnp.concatenate(ys, axis=1) * ps_ref[...]
    o_ref[0] = x_ref[0] + g1_ref[0] * y


def _halo_specs(tm, l, d):
    per = tm // HALO
    last = l // HALO - 1
    prev = pl.BlockSpec((1, HALO, d), lambda bi, i, *_: (bi, jnp.maximum(i * per - 1, 0), 0))
    nxt = pl.BlockSpec((1, HALO, d), lambda bi, i, *_: (bi, jnp.minimum((i + 1) * per, last), 0))
    return prev, nxt


def _pool_call(x, g, shift, scale, pool_w, layer, pool_scale, g1, tm):
    b, l, d = x.shape
    prev, nxt = _halo_specs(tm, l, d)
    kern = functools.partial(_pool_kernel, tm=tm, seq_len=l)
    return pl.pallas_call(
        kern,
        out_shape=jax.ShapeDtypeStruct((b, l, d), F32),
        grid=(b, l // tm),
        in_specs=[pl.BlockSpec((1, tm, d), lambda bi, i: (bi, i, 0)), prev, nxt,
                  pl.BlockSpec((1, d), lambda bi, i: (0, 0)),
                  pl.BlockSpec((1, 1, d), lambda bi, i: (bi, 0, 0)),
                  pl.BlockSpec((1, 1, d), lambda bi, i: (bi, 0, 0)),
                  pl.BlockSpec((1,) + pool_w.shape[1:], lambda bi, i: (layer, 0, 0, 0)),
                  pl.BlockSpec((1, d), lambda bi, i: (0, 0)),
                  pl.BlockSpec((1, 1, d), lambda bi, i: (bi, 0, 0))],
        out_specs=pl.BlockSpec((1, tm, d), lambda bi, i: (bi, i, 0)),
        scratch_shapes=[pltpu.VMEM((tm + 2 * HALO, d), F32)],
        compiler_params=_cparams(2),
        name="pool_mixer",
    )(x, x, x, g, shift, scale, pool_w, pool_scale, g1)


def _ffn_kernel(x_ref, xp_ref, xn_ref, g_ref, sh_ref, sc_ref, wv_ref, wg_ref, cw_ref, cb_ref, wd_ref,
                g2_ref, fg_ref, o_ref, h_scr, acc_scr, *, tm, seg, final_norm):
    i = pl.program_id(1)
    j = pl.program_id(2)

    @pl.when(j == 0)
    def _():
        g = g_ref[...]
        sh = sh_ref[0]
        sc = sc_ref[0]
        h_scr[HALO:HALO + tm, :] = _norm_mod(x_ref[0], g, sh, sc).astype(BF16)
        hp = _norm_mod(xp_ref[0], g, sh, sc)
        hn = _norm_mod(xn_ref[0], g, sh, sc)
        if seg % tm == 0:
            hp = jnp.where(lax.rem(i * tm, seg) == 0, 0.0, hp)
            hn = jnp.where(lax.rem((i + 1) * tm, seg) == 0, 0.0, hn)
        h_scr[0:HALO, :] = hp.astype(BF16)
        h_scr[HALO + tm:, :] = hn.astype(BF16)
        acc_scr[...] = jnp.zeros_like(acc_scr)

    rows_all = tm + 2 * HALO
    ug = _bdot(h_scr[...], wg_ref[0].astype(BF16))
    uv = _bdot(h_scr[HALO:HALO + tm, :], wv_ref[0].astype(BF16))
    cw = cw_ref[0]
    g_prev = pltpu.roll(ug, 1, axis=0)[HALO:HALO + tm]
    g_next = pltpu.roll(ug, rows_all - 1, axis=0)[HALO:HALO + tm]
    if seg % tm != 0:
        pos = lax.rem(i * tm + lax.broadcasted_iota(jnp.int32, (tm, 1), 0), seg)
        g_prev = jnp.where(pos == 0, 0.0, g_prev)
        g_next = jnp.where(pos == seg - 1, 0.0, g_next)
    gate = g_prev * cw[0:1] + ug[HALO:HALO + tm] * cw[1:2] + g_next * cw[2:3] + cb_ref[0]
    act = 0.5 * gate * (1.0 + lax.erf(gate * (2.0 ** -0.5))) * uv
    acc_scr[...] += _bdot(act.astype(BF16), wd_ref[0])

    @pl.when(j == pl.num_programs(2) - 1)
    def _():
        y = x_ref[0] + g2_ref[0] * acc_scr[...]
        if final_norm:
            ms = jnp.mean(y * y, axis=-1, keepdims=True)
            y = y * lax.rsqrt(ms + EPS) * fg_ref[...]
        o_ref[0] = y


def _ffn_call(x, g, shift, scale, w_up, conv_w, conv_b, w_down, layer, g2, final_g, tm, tf, seg, final_norm):
    b, l, d = x.shape
    nf = w_down.shape[1] // tf
    prev, nxt = _halo_specs(tm, l, d)
    kern = functools.partial(_ffn_kernel, tm=tm, seg=seg, final_norm=final_norm)
    return pl.pallas_call(
        kern,
        out_shape=jax.ShapeDtypeStruct((b, l, d), F32),
        grid=(b, l // tm, nf),
        in_specs=[pl.BlockSpec((1, tm, d), lambda bi, i, j: (bi, i, 0)), prev, nxt,
                  pl.BlockSpec((1, d), lambda bi, i, j: (0, 0)),
                  pl.BlockSpec((1, 1, d), lambda bi, i, j: (bi, 0, 0)),
                  pl.BlockSpec((1, 1, d), lambda bi, i, j: (bi, 0, 0)),
                  pl.BlockSpec((1, d, tf), lambda bi, i, j: (layer, 0, j)),
                  pl.BlockSpec((1, d, tf), lambda bi, i, j: (layer, 0, nf + j)),
                  pl.BlockSpec((1, 3, tf), lambda bi, i, j: (layer, 0, j)),
                  pl.BlockSpec((1, 1, tf), lambda bi, i, j: (layer, 0, j)),
                  pl.BlockSpec((1, tf, d), lambda bi, i, j: (layer, j, 0)),
                  pl.BlockSpec((1, 1, d), lambda bi, i, j: (bi, 0, 0)),
                  pl.BlockSpec((1, d), lambda bi, i, j: (0, 0))],
        out_specs=pl.BlockSpec((1, tm, d), lambda bi, i, j: (bi, i, 0)),
        scratch_shapes=[pltpu.VMEM((tm + 2 * HALO, d), BF16), pltpu.VMEM((tm, d), F32)],
        compiler_params=_cparams(3),
        name="conv_ffn",
    )(x, x, x, g, shift, scale, w_up, w_up, conv_w, conv_b, w_down, g2, final_g)


def _tile(l, want):
    return min(l, want)


def _even_mixer(x_lat, x_ctx, mods_lat, mods_ctx, n1, e, w_in, w_in_bf, w_gate2, b_gate, rpb, gla_g, w_out_bf,
                need_ctx, rope_lat, rope_ctx):
    sh, sc, g1 = mods_lat
    shc, scc, gc1 = mods_ctx
    nh = rpb.shape[0]
    a_w = 3 * nh * HEAD_DIM
    n_gla = w_gate2.shape[2] // GLA_DK
    b_w = 2 * n_gla * GLA_DK + 2 * n_gla * GLA_DV
    w_ab = jnp.pad(w_in[:, a_w + b_w:], ((0, 0), (0, 128 - 2 * GLA_RANK))).astype(BF16)
    nqk = n_gla * GLA_DK
    wg2 = jnp.zeros((128, 2 * nqk), F32)
    wg2 = wg2.at[0:GLA_RANK, 0:nqk].set(w_gate2[0]).at[GLA_RANK:2 * GLA_RANK, nqk:].set(w_gate2[1]).astype(BF16)
    bg = b_gate.reshape(1, 2 * nqk)

    l = x_lat.shape[1]
    lc = x_ctx.shape[1]
    bsz, d = x_lat.shape[0], x_lat.shape[2]
    pa, pb, lg = _proj_call(x_lat, n1, sh, sc, w_in_bf, e, a_w, b_w, w_ab, wg2, bg, _tile(l, 1024), 512)
    ctx_flat = x_ctx.reshape(1, bsz * lc, d)
    pa_c, pb_c, lg_c = [a.reshape(bsz, lc, -1) for a in
                        _proj_call(ctx_flat, n1, shc[:1], scc[:1], w_in_bf, e, a_w, b_w, w_ab, wg2, bg,
                                   _tile(bsz * lc, 1024), 512)]

    oa = _na_call(pa, pa_c, rpb)

    s0 = jnp.zeros((bsz, nqk // 128, 2 * GLA_DV, 2 * GLA_DK), F32)
    of_c, ob_c, s_f, s_b = _gla_call(pb_c, lg_c, rope_ctx[0], rope_ctx[1], s0, s0, tb=_tile(lc, 512))
    of, ob, _, _ = _gla_call(pb, lg, rope_lat[0], rope_lat[1], s_f, s_b, tb=_tile(l, 512))

    gg = gla_g.reshape(1, -1)
    x_lat = _combine_call(oa, of, ob, pb, gg, w_out_bf, e, x_lat, g1, _tile(l, 512))
    if need_ctx:
        oa_c = _ctx_attn_call(pa_c, nh)
        x_ctx = _combine_call(oa_c, of_c, ob_c, pb_c, gg, w_out_bf, e, x_ctx, gc1, _tile(lc, 512))
    return x_lat, x_ctx


def kernel(x, c, ctx, c_ctx, w_mod, b_mod, norm1_g, norm2_g, w_in, w_gate2, b_gate, rpb, gla_norm_g, w_out,
           pool_w, pool_scale, w_up, conv_w, conv_b, w_down, final_g):
    bsz, seq, d = x.shape
    lc = ctx.shape[1]
    depth = w_mod.shape[0]

    cvec = jnp.zeros((8, d), F32).at[0:bsz].set(c).at[bsz].set(c_ctx)
    mods = _mod_call(cvec, w_mod, b_mod)

    rope_lat = _rope_tables(seq)
    rope_ctx = (jnp.ones((lc, 128), F32), jnp.zeros((lc, 128), F32))

    fg = final_g.reshape(1, d)
    w_in_bf = w_in.astype(BF16)
    w_out_bf = w_out.astype(BF16)
    w_down_bf = w_down.astype(BF16)
    pool_w_bf = pool_w.astype(BF16)
    conv_b3 = conv_b.reshape(depth, 1, -1)
    x_lat, x_ctx = x, ctx
    for i in range(depth):
        is_even = i % 2 == 0
        need_ctx = any(j % 2 == 0 for j in range(i + 1, depth))
        m = mods[i].reshape(8, 6, d)
        lat = [m[0:bsz, k][:, None, :] for k in range(6)]
        cx = [jnp.broadcast_to(m[bsz, k][None, None, :], (bsz, 1, d)) for k in range(6)]
        n1 = norm1_g[i].reshape(1, d)
        n2 = norm2_g[i].reshape(1, d)
        if is_even:
            e = i // 2
            x_lat, x_ctx = _even_mixer(x_lat, x_ctx, lat[0:3], cx[0:3], n1, e, w_in[e], w_in_bf, w_gate2[e], b_gate[e],
                                       rpb[e], gla_norm_g[e], w_out_bf, need_ctx, rope_lat, rope_ctx)
        else:
            o = i // 2
            ps = pool_scale[o].reshape(1, d)
            x_lat = _pool_call(x_lat, n1, lat[0], lat[1], pool_w_bf, o, ps, lat[2], _tile(seq, 512))
            if need_ctx:
                x_ctx = _pool_call(x_ctx, n1, cx[0], cx[1], pool_w_bf, o, ps, cx[2], _tile(lc, 512))
        x_lat = _ffn_call(x_lat, n2, lat[3], lat[4], w_up, conv_w, conv_b3, w_down_bf, i, lat[5], fg,
                          _tile(seq, 512), 512, seg=seq, final_norm=(i == depth - 1))
        if need_ctx:
            ctx_flat = x_ctx.reshape(1, bsz * lc, d)
            x_ctx = _ffn_call(ctx_flat, n2, cx[3][:1], cx[4][:1], w_up, conv_w, conv_b3, w_down_bf, i, cx[5][:1], fg,
                              _tile(bsz * lc, 512), 512, seg=lc, final_norm=False).reshape(bsz, lc, d)
    return x_lat
```

```python
import functools

import numpy as np
import jax
import jax.numpy as jnp
from jax import lax
from jax.experimental import pallas as pl
from jax.experimental.pallas import tpu as pltpu

F32 = jnp.float32
BF16 = jnp.bfloat16

GRID_W = 64
HEAD_DIM = 128
NA_ROWS = 8
NA_COLS = 16
NA_GROUP = 4
GLA_DK = 64
GLA_DV = 128
GLA_RANK = 16
GLA_TAU = 16.0
GLA_CHUNK = 64
POOL_WINDOWS = (2, 4, 8, 16)
ROPE_BASE = 10000.0
EPS = 1e-6
LOG2E = 1.4426950408889634
HALO = 16
VMEM_LIMIT = 56 * 1024 * 1024


def _cparams(n_axes):
    return pltpu.CompilerParams(dimension_semantics=("arbitrary",) * n_axes,
                                vmem_limit_bytes=VMEM_LIMIT)


def _bdot(a, b):
    return jnp.dot(a, b, preferred_element_type=F32)


def _bdot_nt(a, b):
    return lax.dot_general(a, b, (((1,), (1,)), ((), ())), preferred_element_type=F32)


def _vec_spec(vec, d):
    _, row0, per_batch, k = vec
    return pl.BlockSpec((1, 1, d), lambda bi, *_: (row0 + (bi if per_batch else 0), 0, k))


def _norm_mod(x, g, shift, scale):
    ms = jnp.mean(x * x, axis=-1, keepdims=True)
    return x * lax.rsqrt(ms + EPS) * (g * (1.0 + scale)) + shift


def _mod_kernel(c_ref, w_ref, b_ref, o_ref):
    c = c_ref[...]
    s = c / (1.0 + jnp.exp(-c))
    o_ref[0] = _bdot(s.astype(BF16), w_ref[0].astype(BF16)) + b_ref[0]


def _mod_call(cvec, w_mod, b_mod):
    depth, d, n = w_mod.shape
    tn = 1024
    return pl.pallas_call(
        _mod_kernel,
        out_shape=jax.ShapeDtypeStruct((depth, 8, n), F32),
        grid=(depth, n // tn),
        in_specs=[pl.BlockSpec((8, d), lambda l, j: (0, 0)),
                  pl.BlockSpec((1, d, tn), lambda l, j: (l, 0, j)),
                  pl.BlockSpec((1, 1, tn), lambda l, j: (l, 0, j))],
        out_specs=pl.BlockSpec((1, 8, tn), lambda l, j: (l, 0, j)),
        compiler_params=_cparams(2),
        name="mod_matvec",
    )(cvec, w_mod, b_mod.reshape(depth, 1, n))


def _proj_kernel(x_ref, g_ref, sh_ref, sc_ref, wa_ref, wb_ref, wab_ref, wg2_ref, bg_ref,
                 oa_ref, ob_ref, lg_ref, h_scr, *, na_tiles):
    j = pl.program_id(2)

    @pl.when(j == 0)
    def _():
        h = _norm_mod(x_ref[0], g_ref[...], sh_ref[0], sc_ref[0]).astype(BF16)
        h_scr[...] = h
        ab = _bdot(h, wab_ref[...])
        z = _bdot(ab.astype(BF16), wg2_ref[...]) + bg_ref[...]
        log_sig = jnp.minimum(z, 0.0) - jnp.log1p(jnp.exp(-jnp.abs(z)))
        lg_ref[0] = log_sig * (1.0 / GLA_TAU)

    @pl.when(j < na_tiles)
    def _():
        oa_ref[0] = _bdot(h_scr[...], wa_ref[0]).astype(oa_ref.dtype)

    @pl.when(j >= na_tiles)
    def _():
        ob_ref[0] = _bdot(h_scr[...], wb_ref[0])


def _proj_call(x, g, shift, scale, w_in, layer, na, nb_, wab, wg2, bg, tm, tn):
    b, l, d = x.shape
    ta, tb_ = na // tn, nb_ // tn
    ng = wg2.shape[1]
    kern = functools.partial(_proj_kernel, na_tiles=ta)
    return pl.pallas_call(
        kern,
        out_shape=(jax.ShapeDtypeStruct((b, l, na), BF16), jax.ShapeDtypeStruct((b, l, nb_), F32),
                   jax.ShapeDtypeStruct((b, l, ng), F32)),
        grid=(b, l // tm, ta + tb_),
        in_specs=[pl.BlockSpec((1, tm, d), lambda bi, i, j: (bi, i, 0)),
                  pl.BlockSpec((1, d), lambda bi, i, j: (0, 0)),
                  _vec_spec(shift, d), _vec_spec(scale, d),
                  pl.BlockSpec((1, d, tn), lambda bi, i, j: (layer, 0, jnp.minimum(j, ta - 1))),
                  pl.BlockSpec((1, d, tn), lambda bi, i, j: (layer, 0, jnp.maximum(j, ta))),
                  pl.BlockSpec(wab.shape, lambda bi, i, j: (0, 0)),
                  pl.BlockSpec(wg2.shape, lambda bi, i, j: (0, 0)),
                  pl.BlockSpec((1, ng), lambda bi, i, j: (0, 0))],
        out_specs=(pl.BlockSpec((1, tm, tn), lambda bi, i, j: (bi, i, jnp.minimum(j, ta - 1))),
                   pl.BlockSpec((1, tm, tn), lambda bi, i, j: (bi, i, jnp.maximum(j - ta, 0))),
                   pl.BlockSpec((1, tm, ng), lambda bi, i, j: (bi, i, 0))),
        scratch_shapes=[pltpu.VMEM((tm, d), BF16)],
        compiler_params=_cparams(3),
        name="proj_in",
    )(x, g, shift[0], scale[0], w_in, w_in, wab, wg2, bg)


def _na_kernel(rpb_ref, q_ref, k_ref, v_ref, kc_ref, vc_ref, o_ref, tab_ref, *, rows, n_dr, n_dc):
    head = pl.program_id(0)
    scale = HEAD_DIM ** -0.5 * LOG2E
    wr = NA_ROWS

    qc = lax.broadcasted_iota(jnp.int32, (GRID_W, 2 * GRID_W), 0)
    lane = lax.broadcasted_iota(jnp.int32, (GRID_W, 2 * GRID_W), 1)
    kc = lane & (GRID_W - 1)
    dc = jnp.clip(kc - qc + (NA_COLS - 1), 0, n_dc - 1)
    cstart = jnp.clip(qc - NA_COLS // 2, 0, GRID_W - NA_COLS)
    in_win = (kc >= cstart) & (kc < cstart + NA_COLS)
    left = lane < GRID_W
    base_off = head * (n_dr * n_dc)

    @pl.when(pl.program_id(1) == 0)
    def _():
        for d in range(-1, n_dr):
            acc = jnp.zeros((GRID_W, 2 * GRID_W), F32)
            for c in range(n_dc):
                if d < 0:
                    val = rpb_ref[base_off + (d + 1) * n_dc + c]
                elif d + 1 >= n_dr:
                    val = rpb_ref[base_off + d * n_dc + c]
                else:
                    val = jnp.where(left, rpb_ref[base_off + d * n_dc + c], rpb_ref[base_off + (d + 1) * n_dc + c])
                acc = jnp.where(dc == c, val, acc)
            keep = in_win & ~left if d < 0 else (in_win & left if d + 1 >= n_dr else in_win)
            tab_ref[d + 1] = jnp.where(keep, acc * LOG2E, -jnp.inf)

    kctx = kc_ref[0]
    vctx = vc_ref[0]
    grp = NA_GROUP
    union = grp + wr
    neg_inf = jnp.full((GRID_W, 2 * GRID_W), -jnp.inf, F32)

    def scores(gi):
        r0 = gi * grp
        ustart = jnp.clip(r0 - wr // 2, 0, rows - union)
        q_rows = pl.ds(pl.multiple_of(r0 * GRID_W, grp * GRID_W), grp * GRID_W)
        kv_rows = pl.ds(pl.multiple_of(ustart * GRID_W, GRID_W), union * GRID_W)
        q = q_ref[0, q_rows, :]
        kb = k_ref[0, kv_rows, :]
        biases = []
        for g in range(grp):
            rq = r0 + g
            rstart = jnp.clip(rq - wr // 2, 0, rows - wr)
            tiles = []
            for j in range(union // 2):
                ku = ustart + 2 * j
                d = ku - rq + (NA_ROWS - 1)
                ok_l = ((ku >= rstart) & (ku < rstart + wr)).astype(jnp.int32)
                ok_r = ((ku + 1 >= rstart) & (ku + 1 < rstart + wr)).astype(jnp.int32)
                tile = tab_ref[jnp.clip(d, -1, n_dr - 1) + 1]
                tiles.append(jnp.where(jnp.where(left, ok_l, ok_r) > 0, tile, neg_inf))
            biases.append(jnp.concatenate(tiles, axis=1))
        s_loc = _bdot_nt(q, kb) * scale + jnp.concatenate(biases, axis=0)
        s_ctx = _bdot_nt(q, kctx) * scale
        return s_loc, s_ctx, q_rows, kv_rows

    def softmax(s_loc, s_ctx):
        m = jnp.maximum(jnp.max(s_loc, axis=-1, keepdims=True), jnp.max(s_ctx, axis=-1, keepdims=True))
        p_loc = jnp.exp2(s_loc - m)
        p_ctx = jnp.exp2(s_ctx - m)
        denom = jnp.sum(p_loc, axis=-1, keepdims=True) + jnp.sum(p_ctx, axis=-1, keepdims=True)
        return p_loc.astype(BF16), p_ctx.astype(BF16), denom

    def values(p_loc, p_ctx, denom, q_rows, kv_rows):
        o = _bdot(p_loc, v_ref[0, kv_rows, :]) + _bdot(p_ctx, vctx)
        o_ref[0, q_rows, :] = (o / denom).astype(o_ref.dtype)

    def group_pair(t, carry):
        sa = scores(2 * t)
        sb = scores(2 * t + 1)
        pa = softmax(sa[0], sa[1])
        pb = softmax(sb[0], sb[1])
        values(*pa, sa[2], sa[3])
        values(*pb, sb[2], sb[3])
        return carry

    lax.fori_loop(0, rows // (2 * grp), group_pair, 0)


def _na_call(pa, pa_ctx, rpb):
    b, l, _ = pa.shape
    lc = pa_ctx.shape[1]
    nh, n_dr, n_dc = rpb.shape
    rows = l // GRID_W
    kern = functools.partial(_na_kernel, rows=rows, n_dr=n_dr, n_dc=n_dc)
    return pl.pallas_call(
        kern,
        out_shape=jax.ShapeDtypeStruct((b, l, nh * HEAD_DIM), BF16),
        grid=(nh, b),
        in_specs=[pl.BlockSpec(memory_space=pltpu.SMEM),
                  pl.BlockSpec((1, l, HEAD_DIM), lambda h, bi: (bi, 0, h)),
                  pl.BlockSpec((1, l, HEAD_DIM), lambda h, bi: (bi, 0, nh + h)),
                  pl.BlockSpec((1, l, HEAD_DIM), lambda h, bi: (bi, 0, 2 * nh + h)),
                  pl.BlockSpec((1, lc, HEAD_DIM), lambda h, bi: (bi, 0, nh + h)),
                  pl.BlockSpec((1, lc, HEAD_DIM), lambda h, bi: (bi, 0, 2 * nh + h))],
        out_specs=pl.BlockSpec((1, l, HEAD_DIM), lambda h, bi: (bi, 0, h)),
        scratch_shapes=[pltpu.VMEM((n_dr + 1, GRID_W, 2 * GRID_W), F32)],
        compiler_params=_cparams(2),
        name="na_attention",
    )(rpb.reshape(-1), pa, pa, pa, pa_ctx, pa_ctx)


def _ctx_attn_kernel(q_ref, k_ref, v_ref, o_ref):
    q = q_ref[0]
    s = _bdot_nt(q, k_ref[0]) * (HEAD_DIM ** -0.5)
    m = jnp.max(s, axis=-1, keepdims=True)
    p = jnp.exp(s - m)
    denom = jnp.sum(p, axis=-1, keepdims=True)
    o_ref[0] = (_bdot(p.astype(BF16), v_ref[0]) / denom).astype(o_ref.dtype)


def _ctx_attn_call(pa_ctx, nh):
    b, lc, _ = pa_ctx.shape
    return pl.pallas_call(
        _ctx_attn_kernel,
        out_shape=jax.ShapeDtypeStruct((b, lc, nh * HEAD_DIM), BF16),
        grid=(b, nh),
        in_specs=[pl.BlockSpec((1, lc, HEAD_DIM), lambda bi, h: (bi, 0, h)),
                  pl.BlockSpec((1, lc, HEAD_DIM), lambda bi, h: (bi, 0, nh + h)),
                  pl.BlockSpec((1, lc, HEAD_DIM), lambda bi, h: (bi, 0, 2 * nh + h))],
        out_specs=pl.BlockSpec((1, lc, HEAD_DIM), lambda bi, h: (bi, 0, h)),
        compiler_params=_cparams(2),
        name="ctx_attention",
    )(pa_ctx, pa_ctx, pa_ctx)


_GLA_LEVELS = (32, 16, 8, 4, 2, 1)


def _gla_dir_consts(rev):
    c = GLA_CHUNK
    ii = lax.broadcasted_iota(jnp.int32, (c, c), 0)
    jj = lax.broadcasted_iota(jnp.int32, (c, c), 1)
    tri = ((jj >= ii) if rev else (jj <= ii)).astype(BF16)
    i2 = lax.broadcasted_iota(jnp.int32, (c, 2 * c), 0)
    j2 = lax.broadcasted_iota(jnp.int32, (c, 2 * c), 1) & (c - 1)
    masks = [i2 == j2]
    for s in _GLA_LEVELS:
        same = (i2 // (2 * s)) == (j2 // (2 * s))
        qi = i2 % (2 * s)
        kj = j2 % (2 * s)
        masks.append(same & ((qi < s) & (kj >= s) if rev else (qi >= s) & (kj < s)))
    return tri, masks


def _gla_ref_rows(cum, s, rev, sub8):
    c, w = cum.shape
    blk = 2 * s
    off = s - 1 if rev else s
    pieces = []
    if blk >= 8:
        for b0 in range(0, c, blk):
            pieces.append(jnp.broadcast_to(cum[b0 + off:b0 + off + 1, :], (blk, w)))
    else:
        for g0 in range(0, c, 8):
            out = None
            for b0 in range(0, 8, blk):
                cand = jnp.broadcast_to(cum[g0 + b0 + off:g0 + b0 + off + 1, :], (8, w))
                out = cand if out is None else jnp.where(sub8 >= b0, cand, out)
            pieces.append(out)
    return jnp.concatenate(pieces, axis=0)


def _gla_prep(rev, tri, sub8, q, k, g):
    c = GLA_CHUNK
    g1 = g.astype(BF16)
    r1 = g - g1.astype(F32)
    g2 = r1.astype(BF16)
    g3 = (r1 - g2.astype(F32)).astype(BF16)
    cum = (_bdot(tri, g1) + _bdot(tri, g2) + _bdot(tri, g3)) * LOG2E
    end_row = 0 if rev else c - 1
    last = cum[end_row:end_row + 1, :]
    q_in = (q * jnp.exp2(cum)).astype(BF16)
    k_out = (k * jnp.exp2(last - cum)).astype(BF16)
    e_last = jnp.exp2(last)

    qs = [q.astype(BF16)]
    ks = [k.astype(BF16)]
    for s in _GLA_LEVELS:
        dq = cum - _gla_ref_rows(cum, s, rev, sub8)
        e = jnp.exp2(jnp.minimum(dq, -dq))
        qs.append((q * e).astype(BF16))
        ks.append((k * e).astype(BF16))
    return qs, ks, q_in, k_out, e_last


def _gla_scores(masks, left128, qs, ks, n_pairs):
    c = GLA_CHUNK
    attns = [jnp.zeros((c, 2 * c), F32) for _ in range(n_pairs)]
    for lv_i in range(len(masks)):
        for p in range(n_pairs):
            ksl = slice(p * 128, (p + 1) * 128)
            kp = ks[lv_i][:, ksl]
            kbd = jnp.concatenate([jnp.where(left128, kp, 0), jnp.where(left128, 0, kp)], axis=0)
            attns[p] = jnp.where(masks[lv_i], _bdot_nt(qs[lv_i][:, ksl], kbd), attns[p])
    return [a.astype(BF16) for a in attns]


def _gla_finish(left256, st_diag, attns, q_in, k_out, e_last, v, st_ref, n_pairs):
    outs = []
    for p in range(n_pairs):
        ksl = slice(p * 128, (p + 1) * 128)
        vp16 = v[:, p * 256:(p + 1) * 256].astype(BF16)
        vbd = jnp.concatenate([jnp.where(left256, vp16, 0), jnp.where(left256, 0, vp16)], axis=0)
        st = st_ref[p]
        outs.append(_bdot(attns[p], vbd) + _bdot_nt(q_in[:, ksl], st.astype(BF16)))
        upd = lax.dot_general(vp16, k_out[:, ksl], (((0,), (0,)), ((), ())), preferred_element_type=F32)
        st_ref[p] = st * e_last[:, ksl] + jnp.where(st_diag, upd, 0.0)
    return jnp.concatenate(outs, axis=1)


def _gla_kernel(qf_ref, kf_ref, vf_ref, gf_ref, cosf_ref, sinf_ref,
                qr_ref, kr_ref, vr_ref, gr_ref, cosr_ref, sinr_ref, s0f_ref, s0r_ref,
                of_ref, or_ref, sff_ref, sfr_ref, stf_scr, str_scr, *, n_chunks, n_pairs):
    c = GLA_CHUNK
    blk_i = pl.program_id(1)

    @pl.when(blk_i == 0)
    def _():
        stf_scr[...] = s0f_ref[0]
        str_scr[...] = s0r_ref[0]

    l2 = lax.broadcasted_iota(jnp.int32, (c, 2 * c), 1)
    left128 = l2 < c
    lane32 = l2 & 31
    left256 = lax.broadcasted_iota(jnp.int32, (c, 2 * GLA_DV), 1) < GLA_DV
    sr = lax.broadcasted_iota(jnp.int32, (2 * GLA_DV, 2 * GLA_DK), 0)
    sc_ = lax.broadcasted_iota(jnp.int32, (2 * GLA_DV, 2 * GLA_DK), 1)
    st_diag = (sr // GLA_DV) == (sc_ // GLA_DK)
    sub8 = lax.broadcasted_iota(jnp.int32, (8, n_pairs * 128), 0)
    tri_f, masks_f = _gla_dir_consts(False)
    tri_r, masks_r = _gla_dir_consts(True)

    def rope(x, cos, sin):
        outs = []
        for p in range(n_pairs):
            xs = x[:, p * 128:(p + 1) * 128]
            up = pltpu.roll(xs, 16, axis=1)
            dn = pltpu.roll(xs, 112, axis=1)
            sw = jnp.where(lane32 < 16, dn, up)
            outs.append(xs * cos + sw * sin)
        return jnp.concatenate(outs, axis=1)

    def prep(rev, tri, rows, q_ref, k_ref, g_ref, cos_ref, sin_ref):
        cos = cos_ref[rows, :]
        sin = sin_ref[rows, :]
        q = rope(q_ref[0, rows, :], cos, sin) * (GLA_DK ** -0.5)
        k = rope(k_ref[0, rows, :], cos, sin)
        return _gla_prep(rev, tri, sub8, q, k, g_ref[0, rows, :])

    def chunk(ci, carry):
        rows_f = pl.ds(pl.multiple_of(ci * c, c), c)
        rows_r = pl.ds(pl.multiple_of((n_chunks - 1 - ci) * c, c), c)
        qs_f, ks_f, qin_f, kout_f, el_f = prep(False, tri_f, rows_f, qf_ref, kf_ref, gf_ref, cosf_ref, sinf_ref)
        qs_r, ks_r, qin_r, kout_r, el_r = prep(True, tri_r, rows_r, qr_ref, kr_ref, gr_ref, cosr_ref, sinr_ref)
        at_f = _gla_scores(masks_f, left128, qs_f, ks_f, n_pairs)
        at_r = _gla_scores(masks_r, left128, qs_r, ks_r, n_pairs)
        of_ref[0, rows_f, :] = _gla_finish(left256, st_diag, at_f, qin_f, kout_f, el_f, vf_ref[0, rows_f, :],
                                           stf_scr, n_pairs)
        or_ref[0, rows_r, :] = _gla_finish(left256, st_diag, at_r, qin_r, kout_r, el_r, vr_ref[0, rows_r, :],
                                           str_scr, n_pairs)
        return carry

    lax.fori_loop(0, n_chunks, chunk, 0)

    @pl.when(blk_i == pl.num_programs(1) - 1)
    def _():
        sff_ref[0] = stf_scr[...]
        sfr_ref[0] = str_scr[...]


def _gla_call(pb, lg, cos, sin, s0f, s0r, *, tb):
    b, l, _ = pb.shape
    nqk = lg.shape[2] // 2
    n_pairs = nqk // 128
    nv = n_pairs * 2 * GLA_DV
    nb = l // tb
    kern = functools.partial(_gla_kernel, n_chunks=tb // GLA_CHUNK, n_pairs=n_pairs)
    st_shape = (n_pairs, 2 * GLA_DV, 2 * GLA_DK)

    def dir_specs(tok, gate_blk):
        return [pl.BlockSpec((1, tb, nqk), lambda bi, i: (bi, tok(i), 0)),
                pl.BlockSpec((1, tb, nqk), lambda bi, i: (bi, tok(i), 1)),
                pl.BlockSpec((1, tb, nv), lambda bi, i: (bi, tok(i), 1)),
                pl.BlockSpec((1, tb, nqk), lambda bi, i: (bi, tok(i), gate_blk)),
                pl.BlockSpec((tb, 128), lambda bi, i: (tok(i), 0)),
                pl.BlockSpec((tb, 128), lambda bi, i: (tok(i), 0))]

    fwd_tok = lambda i: i
    rev_tok = lambda i: nb - 1 - i
    st_spec = pl.BlockSpec((1,) + st_shape, lambda bi, i: (bi, 0, 0, 0))
    return pl.pallas_call(
        kern,
        out_shape=(jax.ShapeDtypeStruct((b, l, nv), F32), jax.ShapeDtypeStruct((b, l, nv), F32),
                   jax.ShapeDtypeStruct((b,) + st_shape, F32), jax.ShapeDtypeStruct((b,) + st_shape, F32)),
        grid=(b, nb),
        in_specs=dir_specs(fwd_tok, 0) + dir_specs(rev_tok, 1) + [st_spec, st_spec],
        out_specs=(pl.BlockSpec((1, tb, nv), lambda bi, i: (bi, fwd_tok(i), 0)),
                   pl.BlockSpec((1, tb, nv), lambda bi, i: (bi, rev_tok(i), 0)),
                   st_spec, st_spec),
        scratch_shapes=[pltpu.VMEM(st_shape, F32), pltpu.VMEM(st_shape, F32)],
        compiler_params=_cparams(2),
        name="gla_scan",
    )(pb, pb, pb, lg, cos, sin, pb, pb, pb, lg, cos, sin, s0f, s0r)


def _rope_tables(seq_len):
    t = np.arange(seq_len)
    row = (t // GRID_W).astype(np.float32)
    col = (t % GRID_W).astype(np.float32)
    nf = GLA_DK // 4
    inv = np.float32(ROPE_BASE) ** (-np.arange(nf, dtype=np.float32) / np.float32(nf))
    ar = row[:, None] * inv[None, :]
    ac = col[:, None] * inv[None, :]
    cos = np.concatenate([np.cos(ar), np.cos(ar), np.cos(ac), np.cos(ac)], axis=1)
    sin = np.concatenate([-np.sin(ar), np.sin(ar), -np.sin(ac), np.sin(ac)], axis=1)
    return jnp.asarray(np.tile(cos, (1, 2)), F32), jnp.asarray(np.tile(sin, (1, 2)), F32)


def _combine_kernel(oa_ref, of_ref, ob_ref, gb_ref, gg_ref, w_ref, x_ref, g1_ref, o_ref, *, n_heads):
    ob = of_ref[0] + ob_ref[0]
    parts = []
    for h in range(n_heads):
        oh = ob[:, h * GLA_DV:(h + 1) * GLA_DV]
        ms = jnp.mean(oh * oh, axis=-1, keepdims=True)
        parts.append(oh * lax.rsqrt(ms + EPS))
    obn = jnp.concatenate(parts, axis=1) * gg_ref[...]
    gb = gb_ref[0]
    yb = obn * (gb / (1.0 + jnp.exp(-gb)))
    na_w = oa_ref.shape[2]
    y = _bdot(oa_ref[0], w_ref[0, 0:na_w, :]) + _bdot(yb.astype(BF16), w_ref[0, na_w:, :])
    o_ref[0] = x_ref[0] + g1_ref[0] * y


def _combine_call(oa, of, ob, pb, gla_g, w_out, layer, x, g1, tm):
    b, l, d = x.shape
    na_w = oa.shape[2]
    gl_w = of.shape[2]
    n_heads = gl_w // GLA_DV
    gate_blk = (pb.shape[2] - gl_w) // gl_w
    kern = functools.partial(_combine_kernel, n_heads=n_heads)
    return pl.pallas_call(
        kern,
        out_shape=jax.ShapeDtypeStruct((b, l, d), F32),
        grid=(b, l // tm),
        in_specs=[pl.BlockSpec((1, tm, na_w), lambda bi, i: (bi, i, 0)),
                  pl.BlockSpec((1, tm, gl_w), lambda bi, i: (bi, i, 0)),
                  pl.BlockSpec((1, tm, gl_w), lambda bi, i: (bi, i, 0)),
                  pl.BlockSpec((1, tm, gl_w), lambda bi, i: (bi, i, gate_blk)),
                  pl.BlockSpec((1, gl_w), lambda bi, i: (0, 0)),
                  pl.BlockSpec((1,) + w_out.shape[1:], lambda bi, i: (layer, 0, 0)),
                  pl.BlockSpec((1, tm, d), lambda bi, i: (bi, i, 0)),
                  _vec_spec(g1, d)],
        out_specs=pl.BlockSpec((1, tm, d), lambda bi, i: (bi, i, 0)),
        compiler_params=_cparams(2),
        name="combine_out",
    )(oa, of, ob, pb, gla_g, w_out, x, g1[0])


def _pool_kernel(x_ref, xp_ref, xn_ref, g_ref, sh_ref, sc_ref, w_ref, ps_ref, g1_ref, o_ref, h_scr,
                 *, tm, seq_len):
    i = pl.program_id(1)
    n_i = pl.num_programs(1)
    g = g_ref[...]
    sh = sh_ref[0]
    sc = sc_ref[0]
    h_scr[HALO:HALO + tm, :] = _norm_mod(x_ref[0], g, sh, sc)
    h_scr[0:HALO, :] = jnp.where(i > 0, _norm_mod(xp_ref[0], g, sh, sc), 0.0)
    h_scr[HALO + tm:, :] = jnp.where(i < n_i - 1, _norm_mod(xn_ref[0], g, sh, sc), 0.0)

    t = i * tm + lax.broadcasted_iota(jnp.int32, (tm, 1), 0)
    n_ext = tm + 2 * HALO
    grp = h_scr.shape[1] // len(POOL_WINDOWS)
    ys = []
    for gi, w in enumerate(POOL_WINDOWS):
        parts = []
        for c0 in range(gi * grp, (gi + 1) * grp, 128):
            e = h_scr[:, c0:c0 + 128]
            acc = e + pltpu.roll(e, 1, axis=0)
            n = 2
            while n < w:
                acc = pltpu.roll(acc, n // 2, axis=0) + pltpu.roll(acc, n_ext - n // 2, axis=0)
                n *= 2
            parts.append(acc[HALO:HALO + tm])
        cols = slice(gi * grp, (gi + 1) * grp)
        cnt = (jnp.minimum(t + w // 2, seq_len) - jnp.maximum(t - w // 2, 0)).astype(F32)
        pooled = jnp.concatenate(parts, axis=1) / cnt - h_scr[HALO:HALO + tm, cols]
        ys.append(_bdot(pooled.astype(BF16), w_ref[0, gi]))
    y = jnp.concatenate(ys, axis=1) * ps_ref[...]
    o_ref[0] = x_ref[0] + g1_ref[0] * y


def _halo_specs(tm, l, d):
    per = tm // HALO
    last = l // HALO - 1
    prev = pl.BlockSpec((1, HALO, d), lambda bi, i, *_: (bi, jnp.maximum(i * per - 1, 0), 0))
    nxt = pl.BlockSpec((1, HALO, d), lambda bi, i, *_: (bi, jnp.minimum((i + 1) * per, last), 0))
    return prev, nxt


def _pool_call(x, g, shift, scale, pool_w, layer, pool_scale, g1, tm):
    b, l, d = x.shape
    prev, nxt = _halo_specs(tm, l, d)
    kern = functools.partial(_pool_kernel, tm=tm, seq_len=l)
    return pl.pallas_call(
        kern,
        out_shape=jax.ShapeDtypeStruct((b, l, d), F32),
        grid=(b, l // tm),
        in_specs=[pl.BlockSpec((1, tm, d), lambda bi, i: (bi, i, 0)), prev, nxt,
                  pl.BlockSpec((1, d), lambda bi, i: (0, 0)),
                  _vec_spec(shift, d), _vec_spec(scale, d),
                  pl.BlockSpec((1,) + pool_w.shape[1:], lambda bi, i: (layer, 0, 0, 0)),
                  pl.BlockSpec((1, d), lambda bi, i: (0, 0)),
                  _vec_spec(g1, d)],
        out_specs=pl.BlockSpec((1, tm, d), lambda bi, i: (bi, i, 0)),
        scratch_shapes=[pltpu.VMEM((tm + 2 * HALO, d), F32)],
        compiler_params=_cparams(2),
        name="pool_mixer",
    )(x, x, x, g, shift[0], scale[0], pool_w, pool_scale, g1[0])


def _ffn_kernel(x_ref, xp_ref, xn_ref, g_ref, sh_ref, sc_ref, wv_ref, wg_ref, cw_ref, cb_ref, wd_ref,
                g2_ref, fg_ref, o_ref, h_scr, acc_scr, *, tm, seg, final_norm):
    i = pl.program_id(1)
    j = pl.program_id(2)

    @pl.when(j == 0)
    def _():
        g = g_ref[...]
        sh = sh_ref[0]
        sc = sc_ref[0]
        h_scr[HALO:HALO + tm, :] = _norm_mod(x_ref[0], g, sh, sc).astype(BF16)
        hp = _norm_mod(xp_ref[0], g, sh, sc)
        hn = _norm_mod(xn_ref[0], g, sh, sc)
        if seg % tm == 0:
            hp = jnp.where(lax.rem(i * tm, seg) == 0, 0.0, hp)
            hn = jnp.where(lax.rem((i + 1) * tm, seg) == 0, 0.0, hn)
        h_scr[0:HALO, :] = hp.astype(BF16)
        h_scr[HALO + tm:, :] = hn.astype(BF16)
        acc_scr[...] = jnp.zeros_like(acc_scr)

    rows_all = tm + 2 * HALO
    ug = _bdot(h_scr[...], wg_ref[0])
    uv = _bdot(h_scr[HALO:HALO + tm, :], wv_ref[0])
    cw = cw_ref[0]
    g_prev = pltpu.roll(ug, 1, axis=0)[HALO:HALO + tm]
    g_next = pltpu.roll(ug, rows_all - 1, axis=0)[HALO:HALO + tm]
    if seg % tm != 0:
        pos = lax.rem(i * tm + lax.broadcasted_iota(jnp.int32, (tm, 1), 0), seg)
        g_prev = jnp.where(pos == 0, 0.0, g_prev)
        g_next = jnp.where(pos == seg - 1, 0.0, g_next)
    gate = g_prev * cw[0:1] + ug[HALO:HALO + tm] * cw[1:2] + g_next * cw[2:3] + cb_ref[0]
    act = 0.5 * gate * (1.0 + lax.erf(gate * (2.0 ** -0.5))) * uv
    acc_scr[...] += _bdot(act.astype(BF16), wd_ref[0].astype(BF16))

    @pl.when(j == pl.num_programs(2) - 1)
    def _():
        y = x_ref[0] + g2_ref[0] * acc_scr[...]
        if final_norm:
            ms = jnp.mean(y * y, axis=-1, keepdims=True)
            y = y * lax.rsqrt(ms + EPS) * fg_ref[...]
        o_ref[0] = y


def _ffn_call(x, g, shift, scale, w_up, conv_w, conv_b, w_down, layer, g2, final_g, tm, tf, seg, final_norm):
    b, l, d = x.shape
    nf = w_down.shape[1] // tf
    prev, nxt = _halo_specs(tm, l, d)
    kern = functools.partial(_ffn_kernel, tm=tm, seg=seg, final_norm=final_norm)
    return pl.pallas_call(
        kern,
        out_shape=jax.ShapeDtypeStruct((b, l, d), F32),
        grid=(b, l // tm, nf),
        in_specs=[pl.BlockSpec((1, tm, d), lambda bi, i, j: (bi, i, 0)), prev, nxt,
                  pl.BlockSpec((1, d), lambda bi, i, j: (0, 0)),
                  _vec_spec(shift, d), _vec_spec(scale, d),
                  pl.BlockSpec((1, d, tf), lambda bi, i, j: (layer, 0, j)),
                  pl.BlockSpec((1, d, tf), lambda bi, i, j: (layer, 0, nf + j)),
                  pl.BlockSpec((1, 3, tf), lambda bi, i, j: (layer, 0, j)),
                  pl.BlockSpec((1, 1, tf), lambda bi, i, j: (layer, 0, j)),
                  pl.BlockSpec((1, tf, d), lambda bi, i, j: (layer, j, 0)),
                  _vec_spec(g2, d),
                  pl.BlockSpec((1, d), lambda bi, i, j: (0, 0))],
        out_specs=pl.BlockSpec((1, tm, d), lambda bi, i, j: (bi, i, 0)),
        scratch_shapes=[pltpu.VMEM((tm + 2 * HALO, d), BF16), pltpu.VMEM((tm, d), F32)],
        compiler_params=_cparams(3),
        name="conv_ffn",
    )(x, x, x, g, shift[0], scale[0], w_up, w_up, conv_w, conv_b, w_down, g2[0], final_g)


def _tile(l, want):
    return min(l, want)


def _even_mixer(x_lat, x_ctx, mods_lat, mods_ctx, n1, e, w_in, w_in_bf, w_gate2, b_gate, rpb, gla_g, w_out_bf,
                need_ctx, rope_lat, rope_ctx):
    sh, sc, g1 = mods_lat
    shc, scc, gc1 = mods_ctx
    nh = rpb.shape[0]
    a_w = 3 * nh * HEAD_DIM
    n_gla = w_gate2.shape[2] // GLA_DK
    b_w = 2 * n_gla * GLA_DK + 2 * n_gla * GLA_DV
    w_ab = jnp.pad(w_in[:, a_w + b_w:], ((0, 0), (0, 128 - 2 * GLA_RANK))).astype(BF16)
    nqk = n_gla * GLA_DK
    wg2 = jnp.zeros((128, 2 * nqk), F32)
    wg2 = wg2.at[0:GLA_RANK, 0:nqk].set(w_gate2[0]).at[GLA_RANK:2 * GLA_RANK, nqk:].set(w_gate2[1]).astype(BF16)
    bg = b_gate.reshape(1, 2 * nqk)

    l = x_lat.shape[1]
    lc = x_ctx.shape[1]
    bsz, d = x_lat.shape[0], x_lat.shape[2]
    pa, pb, lg = _proj_call(x_lat, n1, sh, sc, w_in_bf, e, a_w, b_w, w_ab, wg2, bg, _tile(l, 1024), 512)
    ctx_flat = x_ctx.reshape(1, bsz * lc, d)
    pa_c, pb_c, lg_c = [a.reshape(bsz, lc, -1) for a in
                        _proj_call(ctx_flat, n1, shc, scc, w_in_bf, e, a_w, b_w, w_ab, wg2, bg,
                                   _tile(bsz * lc, 1024), 512)]

    oa = _na_call(pa, pa_c, rpb)

    s0 = jnp.zeros((bsz, nqk // 128, 2 * GLA_DV, 2 * GLA_DK), F32)
    of_c, ob_c, s_f, s_b = _gla_call(pb_c, lg_c, rope_ctx[0], rope_ctx[1], s0, s0, tb=_tile(lc, 512))
    of, ob, _, _ = _gla_call(pb, lg, rope_lat[0], rope_lat[1], s_f, s_b, tb=_tile(l, 512))

    gg = gla_g.reshape(1, -1)
    x_lat = _combine_call(oa, of, ob, pb, gg, w_out_bf, e, x_lat, g1, _tile(l, 512))
    if need_ctx:
        oa_c = _ctx_attn_call(pa_c, nh)
        x_ctx = _combine_call(oa_c, of_c, ob_c, pb_c, gg, w_out_bf, e, x_ctx, gc1, _tile(lc, 512))
    return x_lat, x_ctx


def kernel(x, c, ctx, c_ctx, w_mod, b_mod, norm1_g, norm2_g, w_in, w_gate2, b_gate, rpb, gla_norm_g, w_out,
           pool_w, pool_scale, w_up, conv_w, conv_b, w_down, final_g):
    bsz, seq, d = x.shape
    lc = ctx.shape[1]
    depth = w_mod.shape[0]

    cvec = jnp.zeros((8, d), F32).at[0:bsz].set(c).at[bsz].set(c_ctx)
    mods = _mod_call(cvec, w_mod, b_mod).reshape(depth * 8, 1, 6 * d)

    rope_lat = _rope_tables(seq)
    rope_ctx = (jnp.ones((lc, 128), F32), jnp.zeros((lc, 128), F32))

    fg = final_g.reshape(1, d)
    w_in_bf = w_in.astype(BF16)
    w_out_bf = w_out.astype(BF16)
    w_up_bf = w_up.astype(BF16)
    pool_w_bf = pool_w.astype(BF16)
    conv_b3 = conv_b.reshape(depth, 1, -1)
    x_lat, x_ctx = x, ctx
    for i in range(depth):
        is_even = i % 2 == 0
        need_ctx = any(j % 2 == 0 for j in range(i + 1, depth))
        lat = [(mods, i * 8, True, k) for k in range(6)]
        cx = [(mods, i * 8 + bsz, False, k) for k in range(6)]
        n1 = norm1_g[i].reshape(1, d)
        n2 = norm2_g[i].reshape(1, d)
        if is_even:
            e = i // 2
            x_lat, x_ctx = _even_mixer(x_lat, x_ctx, lat[0:3], cx[0:3], n1, e, w_in[e], w_in_bf, w_gate2[e], b_gate[e],
                                       rpb[e], gla_norm_g[e], w_out_bf, need_ctx, rope_lat, rope_ctx)
        else:
            o = i // 2
            ps = pool_scale[o].reshape(1, d)
            x_lat = _pool_call(x_lat, n1, lat[0], lat[1], pool_w_bf, o, ps, lat[2], _tile(seq, 512))
            if need_ctx:
                x_ctx = _pool_call(x_ctx, n1, cx[0], cx[1], pool_w_bf, o, ps, cx[2], _tile(lc, 512))
        x_lat = _ffn_call(x_lat, n2, lat[3], lat[4], w_up_bf, conv_w, conv_b3, w_down, i, lat[5], fg,
                          _tile(seq, 512), 512, seg=seq, final_norm=(i == depth - 1))
        if need_ctx:
            ctx_flat = x_ctx.reshape(1, bsz * lc, d)
            x_ctx = _ffn_call(ctx_flat, n2, cx[3], cx[4], w_up_bf, conv_w, conv_b3, w_down, i, cx[5], fg,
                              _tile(bsz * lc, 512), 512, seg=lc, final_norm=False).reshape(bsz, lc, d)
    return x_lat
```

```python
import functools

import numpy as np
import jax
import jax.numpy as jnp
from jax import lax
from jax.experimental import pallas as pl
from jax.experimental.pallas import tpu as pltpu

F32 = jnp.float32
BF16 = jnp.bfloat16

GRID_W = 64
HEAD_DIM = 128
NA_ROWS = 8
NA_COLS = 16
NA_GROUP = 4
GLA_DK = 64
GLA_DV = 128
GLA_RANK = 16
GLA_TAU = 16.0
GLA_CHUNK = 64
POOL_WINDOWS = (2, 4, 8, 16)
ROPE_BASE = 10000.0
EPS = 1e-6
LOG2E = 1.4426950408889634
HALO = 16
VMEM_LIMIT = 56 * 1024 * 1024


def _cparams(n_axes):
    return pltpu.CompilerParams(dimension_semantics=("arbitrary",) * n_axes,
                                vmem_limit_bytes=VMEM_LIMIT)


def _bdot(a, b):
    return jnp.dot(a, b, preferred_element_type=F32)


def _bdot_nt(a, b):
    return lax.dot_general(a, b, (((1,), (1,)), ((), ())), preferred_element_type=F32)


def _vec_spec(vec, d):
    _, row0, per_batch, k = vec
    return pl.BlockSpec((1, 1, d), lambda bi, *_: (row0 + (bi if per_batch else 0), 0, k))


def _norm_mod(x, g, shift, scale):
    ms = jnp.mean(x * x, axis=-1, keepdims=True)
    return x * lax.rsqrt(ms + EPS) * (g * (1.0 + scale)) + shift


def _mod_kernel(c_ref, w_ref, b_ref, o_ref):
    c = c_ref[...]
    s = c / (1.0 + jnp.exp(-c))
    o_ref[0] = _bdot(s.astype(BF16), w_ref[0].astype(BF16)) + b_ref[0]


def _mod_call(cvec, w_mod, b_mod):
    depth, d, n = w_mod.shape
    tn = 1024
    return pl.pallas_call(
        _mod_kernel,
        out_shape=jax.ShapeDtypeStruct((depth, 8, n), F32),
        grid=(depth, n // tn),
        in_specs=[pl.BlockSpec((8, d), lambda l, j: (0, 0)),
                  pl.BlockSpec((1, d, tn), lambda l, j: (l, 0, j)),
                  pl.BlockSpec((1, 1, tn), lambda l, j: (l, 0, j))],
        out_specs=pl.BlockSpec((1, 8, tn), lambda l, j: (l, 0, j)),
        compiler_params=_cparams(2),
        name="mod_matvec",
    )(cvec, w_mod, b_mod.reshape(depth, 1, n))


def _proj_kernel(x_ref, g_ref, sh_ref, sc_ref, wa_ref, wb_ref, wab_ref, wg2_ref, bg_ref,
                 oa_ref, ob_ref, lg_ref, h_scr, *, na_tiles):
    j = pl.program_id(2)

    @pl.when(j == 0)
    def _():
        h = _norm_mod(x_ref[0], g_ref[...], sh_ref[0], sc_ref[0]).astype(BF16)
        h_scr[...] = h
        ab = _bdot(h, wab_ref[...])
        z = _bdot(ab.astype(BF16), wg2_ref[...]) + bg_ref[...]
        log_sig = jnp.minimum(z, 0.0) - jnp.log1p(jnp.exp(-jnp.abs(z)))
        lg_ref[0] = log_sig * (1.0 / GLA_TAU)

    @pl.when(j < na_tiles)
    def _():
        oa_ref[0] = _bdot(h_scr[...], wa_ref[0]).astype(oa_ref.dtype)

    @pl.when(j >= na_tiles)
    def _():
        ob_ref[0] = _bdot(h_scr[...], wb_ref[0])


def _proj_call(x, g, shift, scale, w_in, layer, na, nb_, wab, wg2, bg, tm, tn):
    b, l, d = x.shape
    ta, tb_ = na // tn, nb_ // tn
    ng = wg2.shape[1]
    kern = functools.partial(_proj_kernel, na_tiles=ta)
    return pl.pallas_call(
        kern,
        out_shape=(jax.ShapeDtypeStruct((b, l, na), BF16), jax.ShapeDtypeStruct((b, l, nb_), F32),
                   jax.ShapeDtypeStruct((b, l, ng), F32)),
        grid=(b, l // tm, ta + tb_),
        in_specs=[pl.BlockSpec((1, tm, d), lambda bi, i, j: (bi, i, 0)),
                  pl.BlockSpec((1, d), lambda bi, i, j: (0, 0)),
                  _vec_spec(shift, d), _vec_spec(scale, d),
                  pl.BlockSpec((1, d, tn), lambda bi, i, j: (layer, 0, jnp.minimum(j, ta - 1))),
                  pl.BlockSpec((1, d, tn), lambda bi, i, j: (layer, 0, jnp.maximum(j, ta))),
                  pl.BlockSpec(wab.shape, lambda bi, i, j: (0, 0)),
                  pl.BlockSpec(wg2.shape, lambda bi, i, j: (0, 0)),
                  pl.BlockSpec((1, ng), lambda bi, i, j: (0, 0))],
        out_specs=(pl.BlockSpec((1, tm, tn), lambda bi, i, j: (bi, i, jnp.minimum(j, ta - 1))),
                   pl.BlockSpec((1, tm, tn), lambda bi, i, j: (bi, i, jnp.maximum(j - ta, 0))),
                   pl.BlockSpec((1, tm, ng), lambda bi, i, j: (bi, i, 0))),
        scratch_shapes=[pltpu.VMEM((tm, d), BF16)],
        compiler_params=_cparams(3),
        name="proj_in",
    )(x, g, shift[0], scale[0], w_in, w_in, wab, wg2, bg)


def _na_kernel(rpb_ref, q_ref, k_ref, v_ref, kc_ref, vc_ref, o_ref, tab_ref, *, rows, n_dr, n_dc):
    head = pl.program_id(0)
    scale = HEAD_DIM ** -0.5 * LOG2E
    wr = NA_ROWS

    qc = lax.broadcasted_iota(jnp.int32, (GRID_W, 2 * GRID_W), 0)
    lane = lax.broadcasted_iota(jnp.int32, (GRID_W, 2 * GRID_W), 1)
    kc = lane & (GRID_W - 1)
    dc = jnp.clip(kc - qc + (NA_COLS - 1), 0, n_dc - 1)
    cstart = jnp.clip(qc - NA_COLS // 2, 0, GRID_W - NA_COLS)
    in_win = (kc >= cstart) & (kc < cstart + NA_COLS)
    left = lane < GRID_W
    base_off = head * (n_dr * n_dc)

    @pl.when(pl.program_id(1) == 0)
    def _():
        for d in range(-1, n_dr):
            acc = jnp.zeros((GRID_W, 2 * GRID_W), F32)
            for c in range(n_dc):
                if d < 0:
                    val = rpb_ref[base_off + (d + 1) * n_dc + c]
                elif d + 1 >= n_dr:
                    val = rpb_ref[base_off + d * n_dc + c]
                else:
                    val = jnp.where(left, rpb_ref[base_off + d * n_dc + c], rpb_ref[base_off + (d + 1) * n_dc + c])
                acc = jnp.where(dc == c, val, acc)
            keep = in_win & ~left if d < 0 else (in_win & left if d + 1 >= n_dr else in_win)
            tab_ref[d + 1] = jnp.where(keep, acc * LOG2E, -jnp.inf)

    kctx = kc_ref[0]
    vctx = vc_ref[0]
    grp = NA_GROUP
    union = grp + wr
    neg_inf = jnp.full((GRID_W, 2 * GRID_W), -jnp.inf, F32)

    def scores(gi):
        r0 = gi * grp
        ustart = jnp.clip(r0 - wr // 2, 0, rows - union)
        q_rows = pl.ds(pl.multiple_of(r0 * GRID_W, grp * GRID_W), grp * GRID_W)
        kv_rows = pl.ds(pl.multiple_of(ustart * GRID_W, GRID_W), union * GRID_W)
        q = q_ref[0, q_rows, :]
        kb = k_ref[0, kv_rows, :]
        biases = []
        for g in range(grp):
            rq = r0 + g
            rstart = jnp.clip(rq - wr // 2, 0, rows - wr)
            tiles = []
            for j in range(union // 2):
                ku = ustart + 2 * j
                d = ku - rq + (NA_ROWS - 1)
                ok_l = ((ku >= rstart) & (ku < rstart + wr)).astype(jnp.int32)
                ok_r = ((ku + 1 >= rstart) & (ku + 1 < rstart + wr)).astype(jnp.int32)
                tile = tab_ref[jnp.clip(d, -1, n_dr - 1) + 1]
                tiles.append(jnp.where(jnp.where(left, ok_l, ok_r) > 0, tile, neg_inf))
            biases.append(jnp.concatenate(tiles, axis=1))
        s_loc = _bdot_nt(q, kb) * scale + jnp.concatenate(biases, axis=0)
        s_ctx = _bdot_nt(q, kctx) * scale
        return s_loc, s_ctx, q_rows, kv_rows

    def softmax(s_loc, s_ctx):
        m = jnp.maximum(jnp.max(s_loc, axis=-1, keepdims=True), jnp.max(s_ctx, axis=-1, keepdims=True))
        p_loc = jnp.exp2(s_loc - m)
        p_ctx = jnp.exp2(s_ctx - m)
        denom = jnp.sum(p_loc, axis=-1, keepdims=True) + jnp.sum(p_ctx, axis=-1, keepdims=True)
        return p_loc.astype(BF16), p_ctx.astype(BF16), denom

    def values(p_loc, p_ctx, denom, q_rows, kv_rows):
        o = _bdot(p_loc, v_ref[0, kv_rows, :]) + _bdot(p_ctx, vctx)
        o_ref[0, q_rows, :] = (o / denom).astype(o_ref.dtype)

    def group_pair(t, carry):
        sa = scores(2 * t)
        sb = scores(2 * t + 1)
        pa = softmax(sa[0], sa[1])
        pb = softmax(sb[0], sb[1])
        values(*pa, sa[2], sa[3])
        values(*pb, sb[2], sb[3])
        return carry

    lax.fori_loop(0, rows // (2 * grp), group_pair, 0)


def _na_call(pa, pa_ctx, rpb):
    b, l, _ = pa.shape
    lc = pa_ctx.shape[1]
    nh, n_dr, n_dc = rpb.shape
    rows = l // GRID_W
    kern = functools.partial(_na_kernel, rows=rows, n_dr=n_dr, n_dc=n_dc)
    return pl.pallas_call(
        kern,
        out_shape=jax.ShapeDtypeStruct((b, l, nh * HEAD_DIM), BF16),
        grid=(nh, b),
        in_specs=[pl.BlockSpec(memory_space=pltpu.SMEM),
                  pl.BlockSpec((1, l, HEAD_DIM), lambda h, bi: (bi, 0, h)),
                  pl.BlockSpec((1, l, HEAD_DIM), lambda h, bi: (bi, 0, nh + h)),
                  pl.BlockSpec((1, l, HEAD_DIM), lambda h, bi: (bi, 0, 2 * nh + h)),
                  pl.BlockSpec((1, lc, HEAD_DIM), lambda h, bi: (bi, 0, nh + h)),
                  pl.BlockSpec((1, lc, HEAD_DIM), lambda h, bi: (bi, 0, 2 * nh + h))],
        out_specs=pl.BlockSpec((1, l, HEAD_DIM), lambda h, bi: (bi, 0, h)),
        scratch_shapes=[pltpu.VMEM((n_dr + 1, GRID_W, 2 * GRID_W), F32)],
        compiler_params=_cparams(2),
        name="na_attention",
    )(rpb.reshape(-1), pa, pa, pa, pa_ctx, pa_ctx)


def _ctx_attn_kernel(q_ref, k_ref, v_ref, o_ref):
    q = q_ref[0]
    s = _bdot_nt(q, k_ref[0]) * (HEAD_DIM ** -0.5)
    m = jnp.max(s, axis=-1, keepdims=True)
    p = jnp.exp(s - m)
    denom = jnp.sum(p, axis=-1, keepdims=True)
    o_ref[0] = (_bdot(p.astype(BF16), v_ref[0]) / denom).astype(o_ref.dtype)


def _ctx_attn_call(pa_ctx, nh):
    b, lc, _ = pa_ctx.shape
    return pl.pallas_call(
        _ctx_attn_kernel,
        out_shape=jax.ShapeDtypeStruct((b, lc, nh * HEAD_DIM), BF16),
        grid=(b, nh),
        in_specs=[pl.BlockSpec((1, lc, HEAD_DIM), lambda bi, h: (bi, 0, h)),
                  pl.BlockSpec((1, lc, HEAD_DIM), lambda bi, h: (bi, 0, nh + h)),
                  pl.BlockSpec((1, lc, HEAD_DIM), lambda bi, h: (bi, 0, 2 * nh + h))],
        out_specs=pl.BlockSpec((1, lc, HEAD_DIM), lambda bi, h: (bi, 0, h)),
        compiler_params=_cparams(2),
        name="ctx_attention",
    )(pa_ctx, pa_ctx, pa_ctx)


_GLA_LEVELS = (32, 16, 8, 4, 2, 1)


def _gla_dir_consts(rev):
    c = GLA_CHUNK
    ii = lax.broadcasted_iota(jnp.int32, (c, c), 0)
    jj = lax.broadcasted_iota(jnp.int32, (c, c), 1)
    tri = ((jj >= ii) if rev else (jj <= ii)).astype(BF16)
    i2 = lax.broadcasted_iota(jnp.int32, (c, 2 * c), 0)
    j2 = lax.broadcasted_iota(jnp.int32, (c, 2 * c), 1) & (c - 1)
    masks = [i2 == j2]
    for s in _GLA_LEVELS:
        same = (i2 // (2 * s)) == (j2 // (2 * s))
        qi = i2 % (2 * s)
        kj = j2 % (2 * s)
        masks.append(same & ((qi < s) & (kj >= s) if rev else (qi >= s) & (kj < s)))
    return tri, masks


def _gla_ref_rows(cum, s, rev, sub8):
    c, w = cum.shape
    blk = 2 * s
    off = s - 1 if rev else s
    pieces = []
    if blk >= 8:
        for b0 in range(0, c, blk):
            pieces.append(jnp.broadcast_to(cum[b0 + off:b0 + off + 1, :], (blk, w)))
    else:
        for g0 in range(0, c, 8):
            out = None
            for b0 in range(0, 8, blk):
                cand = jnp.broadcast_to(cum[g0 + b0 + off:g0 + b0 + off + 1, :], (8, w))
                out = cand if out is None else jnp.where(sub8 >= b0, cand, out)
            pieces.append(out)
    return jnp.concatenate(pieces, axis=0)


def _gla_prep(rev, tri, sub8, q, k, g):
    c = GLA_CHUNK
    g1 = g.astype(BF16)
    r1 = g - g1.astype(F32)
    g2 = r1.astype(BF16)
    g3 = (r1 - g2.astype(F32)).astype(BF16)
    cum = (_bdot(tri, g1) + _bdot(tri, g2) + _bdot(tri, g3)) * LOG2E
    end_row = 0 if rev else c - 1
    last = cum[end_row:end_row + 1, :]
    q_in = (q * jnp.exp2(cum)).astype(BF16)
    k_out = (k * jnp.exp2(last - cum)).astype(BF16)
    e_last = jnp.exp2(last)

    qs = [q.astype(BF16)]
    ks = [k.astype(BF16)]
    for s in _GLA_LEVELS:
        dq = cum - _gla_ref_rows(cum, s, rev, sub8)
        e = jnp.exp2(jnp.minimum(dq, -dq))
        qs.append((q * e).astype(BF16))
        ks.append((k * e).astype(BF16))
    return qs, ks, q_in, k_out, e_last


def _gla_scores(masks, left128, qs, ks, n_pairs):
    c = GLA_CHUNK
    attns = [jnp.zeros((c, 2 * c), F32) for _ in range(n_pairs)]
    for lv_i in range(len(masks)):
        for p in range(n_pairs):
            ksl = slice(p * 128, (p + 1) * 128)
            kp = ks[lv_i][:, ksl]
            kbd = jnp.concatenate([jnp.where(left128, kp, 0), jnp.where(left128, 0, kp)], axis=0)
            attns[p] = jnp.where(masks[lv_i], _bdot_nt(qs[lv_i][:, ksl], kbd), attns[p])
    return [a.astype(BF16) for a in attns]


def _gla_finish(left256, st_diag, attns, q_in, k_out, e_last, v, st_ref, n_pairs):
    outs = []
    for p in range(n_pairs):
        ksl = slice(p * 128, (p + 1) * 128)
        vp16 = v[:, p * 256:(p + 1) * 256].astype(BF16)
        vbd = jnp.concatenate([jnp.where(left256, vp16, 0), jnp.where(left256, 0, vp16)], axis=0)
        st = st_ref[p]
        outs.append(_bdot(attns[p], vbd) + _bdot_nt(q_in[:, ksl], st.astype(BF16)))
        upd = lax.dot_general(vp16, k_out[:, ksl], (((0,), (0,)), ((), ())), preferred_element_type=F32)
        st_ref[p] = st * e_last[:, ksl] + jnp.where(st_diag, upd, 0.0)
    return jnp.concatenate(outs, axis=1)


def _gla_kernel(qf_ref, kf_ref, vf_ref, gf_ref, cosf_ref, sinf_ref,
                qr_ref, kr_ref, vr_ref, gr_ref, cosr_ref, sinr_ref, s0f_ref, s0r_ref,
                of_ref, or_ref, sff_ref, sfr_ref, stf_scr, str_scr, *, n_chunks, n_pairs):
    c = GLA_CHUNK
    blk_i = pl.program_id(1)

    @pl.when(blk_i == 0)
    def _():
        stf_scr[...] = s0f_ref[0]
        str_scr[...] = s0r_ref[0]

    l2 = lax.broadcasted_iota(jnp.int32, (c, 2 * c), 1)
    left128 = l2 < c
    lane32 = l2 & 31
    left256 = lax.broadcasted_iota(jnp.int32, (c, 2 * GLA_DV), 1) < GLA_DV
    sr = lax.broadcasted_iota(jnp.int32, (2 * GLA_DV, 2 * GLA_DK), 0)
    sc_ = lax.broadcasted_iota(jnp.int32, (2 * GLA_DV, 2 * GLA_DK), 1)
    st_diag = (sr // GLA_DV) == (sc_ // GLA_DK)
    sub8 = lax.broadcasted_iota(jnp.int32, (8, n_pairs * 128), 0)
    tri_f, masks_f = _gla_dir_consts(False)
    tri_r, masks_r = _gla_dir_consts(True)

    def rope(x, cos, sin):
        outs = []
        for p in range(n_pairs):
            xs = x[:, p * 128:(p + 1) * 128]
            up = pltpu.roll(xs, 16, axis=1)
            dn = pltpu.roll(xs, 112, axis=1)
            sw = jnp.where(lane32 < 16, dn, up)
            outs.append(xs * cos + sw * sin)
        return jnp.concatenate(outs, axis=1)

    def prep(rev, tri, rows, q_ref, k_ref, g_ref, cos_ref, sin_ref):
        cos = cos_ref[rows, :]
        sin = sin_ref[rows, :]
        q = rope(q_ref[0, rows, :], cos, sin) * (GLA_DK ** -0.5)
        k = rope(k_ref[0, rows, :], cos, sin)
        return _gla_prep(rev, tri, sub8, q, k, g_ref[0, rows, :])

    def chunk(ci, carry):
        rows_f = pl.ds(pl.multiple_of(ci * c, c), c)
        rows_r = pl.ds(pl.multiple_of((n_chunks - 1 - ci) * c, c), c)
        qs_f, ks_f, qin_f, kout_f, el_f = prep(False, tri_f, rows_f, qf_ref, kf_ref, gf_ref, cosf_ref, sinf_ref)
        qs_r, ks_r, qin_r, kout_r, el_r = prep(True, tri_r, rows_r, qr_ref, kr_ref, gr_ref, cosr_ref, sinr_ref)
        at_f = _gla_scores(masks_f, left128, qs_f, ks_f, n_pairs)
        at_r = _gla_scores(masks_r, left128, qs_r, ks_r, n_pairs)
        of_ref[0, rows_f, :] = _gla_finish(left256, st_diag, at_f, qin_f, kout_f, el_f, vf_ref[0, rows_f, :],
                                           stf_scr, n_pairs)
        or_ref[0, rows_r, :] = _gla_finish(left256, st_diag, at_r, qin_r, kout_r, el_r, vr_ref[0, rows_r, :],
                                           str_scr, n_pairs)
        return carry

    lax.fori_loop(0, n_chunks, chunk, 0)

    @pl.when(blk_i == pl.num_programs(1) - 1)
    def _():
        sff_ref[0] = stf_scr[...]
        sfr_ref[0] = str_scr[...]


def _gla_call(pb, lg, cos, sin, s0f, s0r, *, tb):
    b, l, _ = pb.shape
    nqk = lg.shape[2] // 2
    n_pairs = nqk // 128
    nv = n_pairs * 2 * GLA_DV
    nb = l // tb
    kern = functools.partial(_gla_kernel, n_chunks=tb // GLA_CHUNK, n_pairs=n_pairs)
    st_shape = (n_pairs, 2 * GLA_DV, 2 * GLA_DK)

    def dir_specs(tok, gate_blk):
        return [pl.BlockSpec((1, tb, nqk), lambda bi, i: (bi, tok(i), 0)),
                pl.BlockSpec((1, tb, nqk), lambda bi, i: (bi, tok(i), 1)),
                pl.BlockSpec((1, tb, nv), lambda bi, i: (bi, tok(i), 1)),
                pl.BlockSpec((1, tb, nqk), lambda bi, i: (bi, tok(i), gate_blk)),
                pl.BlockSpec((tb, 128), lambda bi, i: (tok(i), 0)),
                pl.BlockSpec((tb, 128), lambda bi, i: (tok(i), 0))]

    fwd_tok = lambda i: i
    rev_tok = lambda i: nb - 1 - i
    st_spec = pl.BlockSpec((1,) + st_shape, lambda bi, i: (bi, 0, 0, 0))
    return pl.pallas_call(
        kern,
        out_shape=(jax.ShapeDtypeStruct((b, l, nv), F32), jax.ShapeDtypeStruct((b, l, nv), F32),
                   jax.ShapeDtypeStruct((b,) + st_shape, F32), jax.ShapeDtypeStruct((b,) + st_shape, F32)),
        grid=(b, nb),
        in_specs=dir_specs(fwd_tok, 0) + dir_specs(rev_tok, 1) + [st_spec, st_spec],
        out_specs=(pl.BlockSpec((1, tb, nv), lambda bi, i: (bi, fwd_tok(i), 0)),
                   pl.BlockSpec((1, tb, nv), lambda bi, i: (bi, rev_tok(i), 0)),
                   st_spec, st_spec),
        scratch_shapes=[pltpu.VMEM(st_shape, F32), pltpu.VMEM(st_shape, F32)],
        compiler_params=_cparams(2),
        name="gla_scan",
    )(pb, pb, pb, lg, cos, sin, pb, pb, pb, lg, cos, sin, s0f, s0r)


def _rope_tables(seq_len):
    t = np.arange(seq_len)
    row = (t // GRID_W).astype(np.float32)
    col = (t % GRID_W).astype(np.float32)
    nf = GLA_DK // 4
    inv = np.float32(ROPE_BASE) ** (-np.arange(nf, dtype=np.float32) / np.float32(nf))
    ar = row[:, None] * inv[None, :]
    ac = col[:, None] * inv[None, :]
    cos = np.concatenate([np.cos(ar), np.cos(ar), np.cos(ac), np.cos(ac)], axis=1)
    sin = np.concatenate([-np.sin(ar), np.sin(ar), -np.sin(ac), np.sin(ac)], axis=1)
    return jnp.asarray(np.tile(cos, (1, 2)), F32), jnp.asarray(np.tile(sin, (1, 2)), F32)


def _combine_kernel(oa_ref, of_ref, ob_ref, gb_ref, gg_ref, w_ref, x_ref, g1_ref, o_ref, *, n_heads):
    ob = of_ref[0] + ob_ref[0]
    parts = []
    for h in range(n_heads):
        oh = ob[:, h * GLA_DV:(h + 1) * GLA_DV]
        ms = jnp.mean(oh * oh, axis=-1, keepdims=True)
        parts.append(oh * lax.rsqrt(ms + EPS))
    obn = jnp.concatenate(parts, axis=1) * gg_ref[...]
    gb = gb_ref[0]
    yb = obn * (gb / (1.0 + jnp.exp(-gb)))
    na_w = oa_ref.shape[2]
    y = _bdot(oa_ref[0], w_ref[0, 0:na_w, :]) + _bdot(yb.astype(BF16), w_ref[0, na_w:, :])
    o_ref[0] = x_ref[0] + g1_ref[0] * y


def _combine_call(oa, of, ob, pb, gla_g, w_out, layer, x, g1, tm):
    b, l, d = x.shape
    na_w = oa.shape[2]
    gl_w = of.shape[2]
    n_heads = gl_w // GLA_DV
    gate_blk = (pb.shape[2] - gl_w) // gl_w
    kern = functools.partial(_combine_kernel, n_heads=n_heads)
    return pl.pallas_call(
        kern,
        out_shape=jax.ShapeDtypeStruct((b, l, d), F32),
        grid=(b, l // tm),
        in_specs=[pl.BlockSpec((1, tm, na_w), lambda bi, i: (bi, i, 0)),
                  pl.BlockSpec((1, tm, gl_w), lambda bi, i: (bi, i, 0)),
                  pl.BlockSpec((1, tm, gl_w), lambda bi, i: (bi, i, 0)),
                  pl.BlockSpec((1, tm, gl_w), lambda bi, i: (bi, i, gate_blk)),
                  pl.BlockSpec((1, gl_w), lambda bi, i: (0, 0)),
                  pl.BlockSpec((1,) + w_out.shape[1:], lambda bi, i: (layer, 0, 0)),
                  pl.BlockSpec((1, tm, d), lambda bi, i: (bi, i, 0)),
                  _vec_spec(g1, d)],
        out_specs=pl.BlockSpec((1, tm, d), lambda bi, i: (bi, i, 0)),
        compiler_params=_cparams(2),
        name="combine_out",
    )(oa, of, ob, pb, gla_g, w_out, x, g1[0])


def _pool_kernel(x_ref, xp_ref, xn_ref, g_ref, sh_ref, sc_ref, w_ref, ps_ref, g1_ref, o_ref, h_scr,
                 *, tm, seq_len):
    i = pl.program_id(1)
    n_i = pl.num_programs(1)
    g = g_ref[...]
    sh = sh_ref[0]
    sc = sc_ref[0]
    h_scr[HALO:HALO + tm, :] = _norm_mod(x_ref[0], g, sh, sc)
    h_scr[0:HALO, :] = jnp.where(i > 0, _norm_mod(xp_ref[0], g, sh, sc), 0.0)
    h_scr[HALO + tm:, :] = jnp.where(i < n_i - 1, _norm_mod(xn_ref[0], g, sh, sc), 0.0)

    t = i * tm + lax.broadcasted_iota(jnp.int32, (tm, 1), 0)
    n_ext = tm + 2 * HALO
    grp = h_scr.shape[1] // len(POOL_WINDOWS)
    ys = []
    for gi, w in enumerate(POOL_WINDOWS):
        parts = []
        for c0 in range(gi * grp, (gi + 1) * grp, 128):
            e = h_scr[:, c0:c0 + 128]
            acc = e + pltpu.roll(e, 1, axis=0)
            n = 2
            while n < w:
                acc = pltpu.roll(acc, n // 2, axis=0) + pltpu.roll(acc, n_ext - n // 2, axis=0)
                n *= 2
            parts.append(acc[HALO:HALO + tm])
        cols = slice(gi * grp, (gi + 1) * grp)
        cnt = (jnp.minimum(t + w // 2, seq_len) - jnp.maximum(t - w // 2, 0)).astype(F32)
        pooled = jnp.concatenate(parts, axis=1) / cnt - h_scr[HALO:HALO + tm, cols]
        ys.append(_bdot(pooled.astype(BF16), w_ref[0, gi]))
    y = jnp.concatenate(ys, axis=1) * ps_ref[...]
    o_ref[0] = x_ref[0] + g1_ref[0] * y


def _halo_specs(tm, l, d):
    per = tm // HALO
    last = l // HALO - 1
    prev = pl.BlockSpec((1, HALO, d), lambda bi, i, *_: (bi, jnp.maximum(i * per - 1, 0), 0))
    nxt = pl.BlockSpec((1, HALO, d), lambda bi, i, *_: (bi, jnp.minimum((i + 1) * per, last), 0))
    return prev, nxt


def _pool_call(x, g, shift, scale, pool_w, layer, pool_scale, g1, tm):
    b, l, d = x.shape
    prev, nxt = _halo_specs(tm, l, d)
    kern = functools.partial(_pool_kernel, tm=tm, seq_len=l)
    return pl.pallas_call(
        kern,
        out_shape=jax.ShapeDtypeStruct((b, l, d), F32),
        grid=(b, l // tm),
        in_specs=[pl.BlockSpec((1, tm, d), lambda bi, i: (bi, i, 0)), prev, nxt,
                  pl.BlockSpec((1, d), lambda bi, i: (0, 0)),
                  _vec_spec(shift, d), _vec_spec(scale, d),
                  pl.BlockSpec((1,) + pool_w.shape[1:], lambda bi, i: (layer, 0, 0, 0)),
                  pl.BlockSpec((1, d), lambda bi, i: (0, 0)),
                  _vec_spec(g1, d)],
        out_specs=pl.BlockSpec((1, tm, d), lambda bi, i: (bi, i, 0)),
        scratch_shapes=[pltpu.VMEM((tm + 2 * HALO, d), F32)],
        compiler_params=_cparams(2),
        name="pool_mixer",
    )(x, x, x, g, shift[0], scale[0], pool_w, pool_scale, g1[0])


def _ffn_kernel(x_ref, xp_ref, xn_ref, g_ref, sh_ref, sc_ref, wv_ref, wg_ref, cw_ref, cb_ref, wd_ref,
                g2_ref, fg_ref, o_ref, h_scr, acc_scr, *, tm, seg, final_norm):
    i = pl.program_id(1)
    j = pl.program_id(2)

    @pl.when(j == 0)
    def _():
        g = g_ref[...]
        sh = sh_ref[0]
        sc = sc_ref[0]
        h_scr[HALO:HALO + tm, :] = _norm_mod(x_ref[0], g, sh, sc).astype(BF16)
        hp = _norm_mod(xp_ref[0], g, sh, sc)
        hn = _norm_mod(xn_ref[0], g, sh, sc)
        if seg % tm == 0:
            hp = jnp.where(lax.rem(i * tm, seg) == 0, 0.0, hp)
            hn = jnp.where(lax.rem((i + 1) * tm, seg) == 0, 0.0, hn)
        h_scr[0:HALO, :] = hp.astype(BF16)
        h_scr[HALO + tm:, :] = hn.astype(BF16)
        acc_scr[...] = jnp.zeros_like(acc_scr)

    rows_all = tm + 2 * HALO
    ug = _bdot(h_scr[...], wg_ref[0])
    uv = _bdot(h_scr[HALO:HALO + tm, :], wv_ref[0])
    cw = cw_ref[0]
    g_prev = pltpu.roll(ug, 1, axis=0)[HALO:HALO + tm]
    g_next = pltpu.roll(ug, rows_all - 1, axis=0)[HALO:HALO + tm]
    if seg % tm != 0:
        pos = lax.rem(i * tm + lax.broadcasted_iota(jnp.int32, (tm, 1), 0), seg)
        g_prev = jnp.where(pos == 0, 0.0, g_prev)
        g_next = jnp.where(pos == seg - 1, 0.0, g_next)
    gate = g_prev * cw[0:1] + ug[HALO:HALO + tm] * cw[1:2] + g_next * cw[2:3] + cb_ref[0]
    act = 0.5 * gate * (1.0 + lax.erf(gate * (2.0 ** -0.5))) * uv
    acc_scr[...] += _bdot(act.astype(BF16), wd_ref[0].astype(BF16))

    @pl.when(j == pl.num_programs(2) - 1)
    def _():
        y = x_ref[0] + g2_ref[0] * acc_scr[...]
        if final_norm:
            ms = jnp.mean(y * y, axis=-1, keepdims=True)
            y = y * lax.rsqrt(ms + EPS) * fg_ref[...]
        o_ref[0] = y


def _ffn_call(x, g, shift, scale, w_up, conv_w, conv_b, w_down, layer, g2, final_g, tm, tf, seg, final_norm):
    b, l, d = x.shape
    nf = w_down.shape[1] // tf
    prev, nxt = _halo_specs(tm, l, d)
    kern = functools.partial(_ffn_kernel, tm=tm, seg=seg, final_norm=final_norm)
    return pl.pallas_call(
        kern,
        out_shape=jax.ShapeDtypeStruct((b, l, d), F32),
        grid=(b, l // tm, nf),
        in_specs=[pl.BlockSpec((1, tm, d), lambda bi, i, j: (bi, i, 0)), prev, nxt,
                  pl.BlockSpec((1, d), lambda bi, i, j: (0, 0)),
                  _vec_spec(shift, d), _vec_spec(scale, d),
                  pl.BlockSpec((1, d, tf), lambda bi, i, j: (layer, 0, j)),
                  pl.BlockSpec((1, d, tf), lambda bi, i, j: (layer, 0, nf + j)),
                  pl.BlockSpec((1, 3, tf), lambda bi, i, j: (layer, 0, j)),
                  pl.BlockSpec((1, 1, tf), lambda bi, i, j: (layer, 0, j)),
                  pl.BlockSpec((1, tf, d), lambda bi, i, j: (layer, j, 0)),
                  _vec_spec(g2, d),
                  pl.BlockSpec((1, d), lambda bi, i, j: (0, 0))],
        out_specs=pl.BlockSpec((1, tm, d), lambda bi, i, j: (bi, i, 0)),
        scratch_shapes=[pltpu.VMEM((tm + 2 * HALO, d), BF16), pltpu.VMEM((tm, d), F32)],
        compiler_params=_cparams(3),
        name="conv_ffn",
    )(x, x, x, g, shift[0], scale[0], w_up, w_up, conv_w, conv_b, w_down, g2[0], final_g)


def _tile(l, want):
    return min(l, want)


def _even_mixer(x_lat, x_ctx, mods_lat, mods_ctx, n1, e, w_in, w_in_bf, w_gate2, b_gate, rpb, gla_g, w_out_bf,
                need_ctx, rope_lat, rope_ctx):
    sh, sc, g1 = mods_lat
    shc, scc, gc1 = mods_ctx
    nh = rpb.shape[0]
    a_w = 3 * nh * HEAD_DIM
    n_gla = w_gate2.shape[2] // GLA_DK
    b_w = 2 * n_gla * GLA_DK + 2 * n_gla * GLA_DV
    w_ab = jnp.pad(w_in[:, a_w + b_w:], ((0, 0), (0, 128 - 2 * GLA_RANK))).astype(BF16)
    nqk = n_gla * GLA_DK
    wg2 = jnp.zeros((128, 2 * nqk), F32)
    wg2 = wg2.at[0:GLA_RANK, 0:nqk].set(w_gate2[0]).at[GLA_RANK:2 * GLA_RANK, nqk:].set(w_gate2[1]).astype(BF16)
    bg = b_gate.reshape(1, 2 * nqk)

    l = x_lat.shape[1]
    lc = x_ctx.shape[1]
    bsz, d = x_lat.shape[0], x_lat.shape[2]
    pa, pb, lg = _proj_call(x_lat, n1, sh, sc, w_in_bf, e, a_w, b_w, w_ab, wg2, bg, _tile(l, 1024), 768)
    ctx_flat = x_ctx.reshape(1, bsz * lc, d)
    pa_c, pb_c, lg_c = [a.reshape(bsz, lc, -1) for a in
                        _proj_call(ctx_flat, n1, shc, scc, w_in_bf, e, a_w, b_w, w_ab, wg2, bg,
                                   _tile(bsz * lc, 1024), 512)]

    oa = _na_call(pa, pa_c, rpb)

    s0 = jnp.zeros((bsz, nqk // 128, 2 * GLA_DV, 2 * GLA_DK), F32)
    of_c, ob_c, s_f, s_b = _gla_call(pb_c, lg_c, rope_ctx[0], rope_ctx[1], s0, s0, tb=_tile(lc, 512))
    of, ob, _, _ = _gla_call(pb, lg, rope_lat[0], rope_lat[1], s_f, s_b, tb=_tile(l, 512))

    gg = gla_g.reshape(1, -1)
    x_lat = _combine_call(oa, of, ob, pb, gg, w_out_bf, e, x_lat, g1, _tile(l, 512))
    if need_ctx:
        oa_c = _ctx_attn_call(pa_c, nh)
        x_ctx = _combine_call(oa_c, of_c, ob_c, pb_c, gg, w_out_bf, e, x_ctx, gc1, _tile(lc, 512))
    return x_lat, x_ctx


def kernel(x, c, ctx, c_ctx, w_mod, b_mod, norm1_g, norm2_g, w_in, w_gate2, b_gate, rpb, gla_norm_g, w_out,
           pool_w, pool_scale, w_up, conv_w, conv_b, w_down, final_g):
    bsz, seq, d = x.shape
    lc = ctx.shape[1]
    depth = w_mod.shape[0]

    cvec = jnp.zeros((8, d), F32).at[0:bsz].set(c).at[bsz].set(c_ctx)
    mods = _mod_call(cvec, w_mod, b_mod).reshape(depth * 8, 1, 6 * d)

    rope_lat = _rope_tables(seq)
    rope_ctx = (jnp.ones((lc, 128), F32), jnp.zeros((lc, 128), F32))

    fg = final_g.reshape(1, d)
    w_in_bf = w_in.astype(BF16)
    w_out_bf = w_out.astype(BF16)
    w_up_bf = w_up.astype(BF16)
    pool_w_bf = pool_w.astype(BF16)
    conv_b3 = conv_b.reshape(depth, 1, -1)
    x_lat, x_ctx = x, ctx
    for i in range(depth):
        is_even = i % 2 == 0
        need_ctx = any(j % 2 == 0 for j in range(i + 1, depth))
        lat = [(mods, i * 8, True, k) for k in range(6)]
        cx = [(mods, i * 8 + bsz, False, k) for k in range(6)]
        n1 = norm1_g[i].reshape(1, d)
        n2 = norm2_g[i].reshape(1, d)
        if is_even:
            e = i // 2
            x_lat, x_ctx = _even_mixer(x_lat, x_ctx, lat[0:3], cx[0:3], n1, e, w_in[e], w_in_bf, w_gate2[e], b_gate[e],
                                       rpb[e], gla_norm_g[e], w_out_bf, need_ctx, rope_lat, rope_ctx)
        else:
            o = i // 2
            ps = pool_scale[o].reshape(1, d)
            x_lat = _pool_call(x_lat, n1, lat[0], lat[1], pool_w_bf, o, ps, lat[2], _tile(seq, 512))
            if need_ctx:
                x_ctx = _pool_call(x_ctx, n1, cx[0], cx[1], pool_w_bf, o, ps, cx[2], _tile(lc, 512))
        x_lat = _ffn_call(x_lat, n2, lat[3], lat[4], w_up_bf, conv_w, conv_b3, w_down, i, lat[5], fg,
                          _tile(seq, 512), 512, seg=seq, final_norm=(i == depth - 1))
        if need_ctx:
            ctx_flat = x_ctx.reshape(1, bsz * lc, d)
            x_ctx = _ffn_call(ctx_flat, n2, cx[3], cx[4], w_up_bf, conv_w, conv_b3, w_down, i, cx[5], fg,
                              _tile(bsz * lc, 512), 512, seg=lc, final_norm=False).reshape(bsz, lc, d)
    return x_lat
```
